```python
import math
import jax, jax.numpy as jnp
from jax import lax
import numpy as np

D_MODEL = 1024
BATCH = 2
SEQ = 16384
DEPTH = 1
DEC_BATCH = 128
DEC_SEQ = 1
PAST_LEN = 8192
PAGE_SIZE = 128

DN_HEADS = 4
DN_DK = 128
DN_DV = 128
CONV_W = 4
CHUNK = 64
AT_HEADS = 8
AT_HD = 64
DIL_PAIRS = ((128, 1), (512, 4), (2048, 16))
WIN_MAX = max(w for w, _ in DIL_PAIRS)
BAND_BLK = 128
NUM_BUCKETS = 32
MAX_DIST = 2048
FFN_HIDDEN = -(-8 * D_MODEL // (3 * 256)) * 256
PLE_DIM = 256
RMS_EPS = 1e-6
DN_QK = DN_HEADS * DN_DK
DN_V = DN_HEADS * DN_DV
CONV_CH = 2 * DN_QK + DN_V
AT_W = AT_HEADS * AT_HD
MIX_W = DN_V + AT_W
IN_SPLITS = (CONV_CH, CONV_CH + DN_V, CONV_CH + DN_V + DN_HEADS, CONV_CH + DN_V + 2 * DN_HEADS,
             CONV_CH + DN_V + 2 * DN_HEADS + AT_W, CONV_CH + DN_V + 2 * DN_HEADS + 2 * AT_W)
IN_COLS = CONV_CH + DN_V + 2 * DN_HEADS + 3 * AT_W

kernel_name = 'hybrid_gated_delta_dilated_swa_step'


def rmsnorm(x, g):
    xf = x.astype(jnp.float32)
    y = xf * lax.rsqrt(jnp.mean(xf * xf, axis=-1, keepdims=True) + RMS_EPS)
    return (y * g.astype(jnp.float32)).astype(x.dtype)


def l2norm(x):
    xf = x.astype(jnp.float32)
    return xf * lax.rsqrt(jnp.sum(xf * xf, axis=-1, keepdims=True) + 1e-6)


def t5_bucket(n):
    max_exact = NUM_BUCKETS // 2
    n = np.asarray(n)
    large = max_exact + (np.log(np.maximum(n, 1) / max_exact) / math.log(MAX_DIST / max_exact)
                         * (NUM_BUCKETS - max_exact)).astype(np.int32)
    large = np.minimum(large, NUM_BUCKETS - 1)
    return np.where(n < max_exact, n, large).astype(np.int32)


def pair_bias(rel_bias, d, nw):
    return rel_bias[t5_bucket(np.arange(nw) * d)].T.astype(jnp.float32)


def short_conv(u, ctx, w):
    t = u.shape[1]
    ucat = jnp.concatenate([ctx.astype(u.dtype), u], axis=1)
    out = ucat[:, 0:t] * w[0]
    for i in range(1, CONV_W):
        out = out + ucat[:, i:i + t] * w[i]
    return out, ucat[:, -(CONV_W - 1):]


def gated_delta_chunked(q, k, v, beta, g):
    q, k, v = (t.astype(jnp.float32) for t in (q, k, v))
    B, S, H, DK = q.shape
    DV = v.shape[-1]
    N = S // CHUNK

    def chunks(t):
        return jnp.moveaxis(t.reshape(B, N, CHUNK, H, -1), 3, 2)

    qc, kc, vc = chunks(q), chunks(k), chunks(v)
    bc = jnp.moveaxis(beta.reshape(B, N, CHUNK, H), 3, 2)
    gc = jnp.cumsum(jnp.moveaxis(g.reshape(B, N, CHUNK, H), 3, 2), axis=-1)
    incl = np.tril(np.ones((CHUNK, CHUNK), bool))
    strict = np.tril(np.ones((CHUNK, CHUNK), bool), -1)
    decay = jnp.exp(jnp.where(incl, gc[..., :, None] - gc[..., None, :], -jnp.inf))
    kb = kc * bc[..., None]
    lower = jnp.where(strict, jnp.einsum('bnhid,bnhjd->bnhij', kb, kc) * decay, 0.0)
    a_mat = lower + jnp.eye(CHUNK, dtype=jnp.float32)
    u = lax.linalg.triangular_solve(a_mat, vc * bc[..., None], left_side=True, lower=True)
    w = lax.linalg.triangular_solve(a_mat, kb * jnp.exp(gc)[..., None], left_side=True, lower=True)
    intra = jnp.where(incl, jnp.einsum('bnhid,bnhjd->bnhij', qc, kc) * decay, 0.0)
    q_dec = qc * jnp.exp(gc)[..., None]
    k_dec = kc * jnp.exp(gc[..., -1:] - gc)[..., None]
    g_last = jnp.exp(gc[..., -1])

    def step(state, xs):
        u_i, w_i, q_i, k_i, a_i, gl_i = xs
        v_new = u_i - jnp.einsum('bhcd,bhde->bhce', w_i, state)
        o_i = jnp.einsum('bhcd,bhde->bhce', q_i, state) + jnp.einsum('bhij,bhje->bhie', a_i, v_new)
        state = state * gl_i[..., None, None] + jnp.einsum('bhcd,bhce->bhde', k_i, v_new)
        return state, o_i

    xs = tuple(jnp.moveaxis(t, 1, 0) for t in (u, w, q_dec, k_dec, intra, g_last))
    s_fin, o = lax.scan(step, jnp.zeros((B, H, DK, DV), jnp.float32), xs)
    o = jnp.moveaxis(jnp.moveaxis(o, 0, 1), 2, 3).reshape(B, S, H, DV)
    return o, s_fin


def gated_delta_recurrent(q, k, v, beta, g, state):
    xs = tuple(jnp.moveaxis(t.astype(jnp.float32), 1, 0) for t in (q, k, v, beta, g))

    def step(s, x):
        q_t, k_t, v_t, b_t, g_t = x
        s = s * jnp.exp(g_t)[..., None, None]
        kv = jnp.einsum('bhd,bhde->bhe', k_t, s)
        delta = (v_t - kv) * b_t[..., None]
        s = s + k_t[..., :, None] * delta[..., None, :]
        return s, jnp.einsum('bhd,bhde->bhe', q_t, s)

    s_fin, o = lax.scan(step, state.astype(jnp.float32), xs)
    return jnp.moveaxis(o, 0, 1), s_fin


def merge_by_denominator(outs, lses):
    wts = jax.nn.softmax(jnp.stack(lses, 0), axis=0)
    return jnp.sum(wts[..., None] * jnp.stack(outs, 0), axis=0)


def band_attention_strided(q, k, v, bias, d, nw):
    B, S, H, E = q.shape
    M = S // d
    nb = -(-M // BAND_BLK)
    Mp = nb * BAND_BLK

    def to_blocks(t):
        t = t.reshape(B, M, d, H, E).transpose(0, 2, 1, 3, 4)
        t = jnp.pad(t, ((0, 0), (0, 0), (0, Mp - M), (0, 0), (0, 0)))
        return t.reshape(B, d, nb, BAND_BLK, H, E)

    def with_prev(t):
        prev = jnp.pad(t, ((0, 0), (0, 0), (1, 0), (0, 0), (0, 0), (0, 0)))[:, :, :-1]
        return jnp.concatenate([prev, t], axis=3)

    qb = to_blocks(q)
    kk = with_prev(to_blocks(k))
    vv = with_prev(to_blocks(v))
    rel = BAND_BLK + np.arange(BAND_BLK)[:, None] - np.arange(2 * BAND_BLK)[None, :]
    m_key = (np.arange(nb)[:, None, None] - 1) * BAND_BLK + np.arange(2 * BAND_BLK)[None, None, :]
    valid = (rel >= 0) & (rel < nw) & (m_key >= 0)
    b = bias[:, np.clip(rel, 0, nw - 1)]
    s = jnp.einsum('bdnqhe,bdnkhe->bdnhqk', qb, kk, preferred_element_type=jnp.float32) * (E ** -0.5) + b
    s = jnp.where(valid[:, None], s, -jnp.inf)
    m = jnp.max(s, axis=-1, keepdims=True)
    pr = jnp.exp(s - m)
    den = jnp.sum(pr, axis=-1)
    o = jnp.einsum('bdnhqk,bdnkhe->bdnqhe', pr, vv.astype(jnp.float32)) / jnp.moveaxis(den, -1, -2)[..., None]
    lse = m[..., 0] + jnp.log(den)
    o = o.reshape(B, d, Mp, H, E)[:, :, :M].transpose(0, 2, 1, 3, 4).reshape(B, S, H, E)
    lse = jnp.moveaxis(lse, -1, -2).reshape(B, d, Mp, H)[:, :, :M].transpose(0, 2, 1, 3).reshape(B, S, H)
    return o, lse


def dilated_attention_prompt(q, k, v, rel_bias):
    outs, lses = [], []
    for w, d in DIL_PAIRS:
        nw = w // d + 1
        o, l = band_attention_strided(q, k, v, pair_bias(rel_bias, d, nw), d, nw)
        outs.append(o)
        lses.append(l)
    n_keep = min(WIN_MAX, q.shape[1])
    return merge_by_denominator(outs, lses), k[:, -n_keep:], v[:, -n_keep:]


def dilated_attention_sample(q, k, v, win_k, win_v, rel_bias):
    W = win_k.shape[1]
    T = q.shape[1]
    E = q.shape[-1]
    kc = jnp.concatenate([win_k.astype(k.dtype), k], axis=1)
    vc = jnp.concatenate([win_v.astype(v.dtype), v], axis=1)
    outs, lses = [], []
    for w, d in DIL_PAIRS:
        nw = w // d + 1
        idx = W + np.arange(T)[:, None] - d * np.arange(nw)[None, :]
        valid = idx >= 0
        idx = np.maximum(idx, 0)
        kg = kc[:, idx]
        vg = vc[:, idx]
        s = jnp.einsum('bthe,btjhe->bthj', q, kg, preferred_element_type=jnp.float32) * (E ** -0.5) + pair_bias(rel_bias, d, nw)
        s = jnp.where(valid[:, None, :], s, -jnp.inf)
        m = jnp.max(s, axis=-1, keepdims=True)
        pr = jnp.exp(s - m)
        den = jnp.sum(pr, axis=-1)
        outs.append(jnp.einsum('bthj,btjhe->bthe', pr, vg.astype(jnp.float32)) / den[..., None])
        lses.append(m[..., 0] + jnp.log(den))
    return merge_by_denominator(outs, lses), kc[:, -W:], vc[:, -W:]


def decoder_layer(x, p, state, rel_bias, g_pre_mix, w_in, conv_w, a_log, dt_bias, g_dn_out, w_out, g_post_mix,
                  g_pre_ffn, w_gate_up, w_down, g_post_ffn, w_ple_proj, w_ple_gate, g_ple):
    B, S, _ = x.shape
    h = rmsnorm(x, g_pre_mix)
    proj = h @ w_in
    qkv_dn, z_dn, b_dn, a_dn, q_at, k_at, v_at = jnp.split(proj, IN_SPLITS, axis=-1)
    conv_ctx = jnp.zeros((B, CONV_W - 1, CONV_CH), x.dtype) if state is None else state[0]
    qkv_c, new_conv = short_conv(qkv_dn, conv_ctx, conv_w)
    qkv_c = jax.nn.silu(qkv_c)
    q, k, v = jnp.split(qkv_c, [DN_QK, 2 * DN_QK], axis=-1)
    q = l2norm(q.reshape(B, S, DN_HEADS, DN_DK)) * (DN_DK ** -0.5)
    k = l2norm(k.reshape(B, S, DN_HEADS, DN_DK))
    v = v.reshape(B, S, DN_HEADS, DN_DV)
    beta = jax.nn.sigmoid(b_dn.astype(jnp.float32))
    g = -jnp.exp(a_log.astype(jnp.float32)) * jax.nn.softplus(a_dn.astype(jnp.float32) + dt_bias.astype(jnp.float32))
    if state is None:
        o_dn, new_delta = gated_delta_chunked(q, k, v, beta, g)
    else:
        o_dn, new_delta = gated_delta_recurrent(q, k, v, beta, g, state[1])
    o_dn = rmsnorm(o_dn.astype(x.dtype), g_dn_out) * jax.nn.silu(z_dn.reshape(B, S, DN_HEADS, DN_DV))
    q_at = q_at.reshape(B, S, AT_HEADS, AT_HD)
    k_at = k_at.reshape(B, S, AT_HEADS, AT_HD)
    v_at = v_at.reshape(B, S, AT_HEADS, AT_HD)
    if state is None:
        o_at, new_wk, new_wv = dilated_attention_prompt(q_at, k_at, v_at, rel_bias)
    else:
        o_at, new_wk, new_wv = dilated_attention_sample(q_at, k_at, v_at, state[2], state[3], rel_bias)
    mix = jnp.concatenate([o_dn.reshape(B, S, DN_V), o_at.astype(x.dtype).reshape(B, S, AT_W)], axis=-1) @ w_out
    x = x + rmsnorm(mix, g_post_mix)
    gt, up = jnp.split(rmsnorm(x, g_pre_ffn) @ w_gate_up, 2, axis=-1)
    x = x + rmsnorm((jax.nn.silu(gt) * up) @ w_down, g_post_ffn)
    x = x + rmsnorm(jax.nn.sigmoid(x @ w_ple_gate) * (p.astype(x.dtype) @ w_ple_proj), g_ple)
    return x, new_conv, new_delta, new_wk, new_wv


def setup_inputs(seed: int = 0) -> dict:
    key = jax.random.key(seed)
    ks = jax.random.split(key, 24)
    f32 = jnp.float32

    def nrm(k, shape, s=1.0):
        return jax.random.normal(k, shape, f32) * s

    def gain(k, shape):
        return 1.0 + 0.02 * jax.random.normal(k, shape, f32)

    w_buf = min(WIN_MAX, PAST_LEN)
    dt = jnp.exp(jax.random.uniform(ks[13], (DEPTH, DN_HEADS), f32, math.log(1e-3), math.log(1e-1)))
    return {
        'x_prompt': nrm(ks[0], (BATCH, SEQ, D_MODEL)),
        'x_sample': nrm(ks[1], (DEC_BATCH, DEC_SEQ, D_MODEL)),
        'cache_conv': nrm(ks[2], (DEPTH, DEC_BATCH, CONV_W - 1, CONV_CH)),
        'state_delta': nrm(ks[3], (DEPTH, DEC_BATCH, DN_HEADS, DN_DK, DN_DV), DN_DK ** -0.5),
        'cache_win_k': nrm(ks[4], (DEPTH, DEC_BATCH, w_buf, AT_HEADS, AT_HD)),
        'cache_win_v': nrm(ks[5], (DEPTH, DEC_BATCH, w_buf, AT_HEADS, AT_HD)),
        'p_prompt': nrm(ks[6], (DEPTH, BATCH, SEQ, PLE_DIM)),
        'p_sample': nrm(ks[7], (DEPTH, DEC_BATCH, DEC_SEQ, PLE_DIM)),
        'rel_bias': nrm(ks[8], (NUM_BUCKETS, AT_HEADS), 0.5),
        'g_pre_mix': gain(ks[9], (DEPTH, D_MODEL)),
        'w_in': nrm(ks[10], (DEPTH, D_MODEL, IN_COLS), D_MODEL ** -0.5),
        'conv_w': nrm(ks[11], (DEPTH, CONV_W, CONV_CH), CONV_W ** -0.5),
        'a_log': jnp.log(jax.random.uniform(ks[12], (DEPTH, DN_HEADS), f32, 1.0, 16.0)),
        'dt_bias': dt + jnp.log(-jnp.expm1(-dt)),
        'g_dn_out': gain(ks[14], (DEPTH, DN_DV)),
        'w_out': nrm(ks[15], (DEPTH, MIX_W, D_MODEL), MIX_W ** -0.5),
        'g_post_mix': gain(ks[16], (DEPTH, D_MODEL)),
        'g_pre_ffn': gain(ks[17], (DEPTH, D_MODEL)),
        'w_gate_up': nrm(ks[18], (DEPTH, D_MODEL, 2 * FFN_HIDDEN), D_MODEL ** -0.5),
        'w_down': nrm(ks[19], (DEPTH, FFN_HIDDEN, D_MODEL), FFN_HIDDEN ** -0.5),
        'g_post_ffn': gain(ks[20], (DEPTH, D_MODEL)),
        'w_ple_proj': nrm(ks[21], (DEPTH, PLE_DIM, D_MODEL), PLE_DIM ** -0.5),
        'w_ple_gate': nrm(ks[22], (DEPTH, D_MODEL, D_MODEL), D_MODEL ** -0.5),
        'g_ple': gain(ks[23], (DEPTH, D_MODEL)),
    }


def reference(x_prompt, x_sample, cache_conv, state_delta, cache_win_k, cache_win_v, p_prompt, p_sample,
              rel_bias, g_pre_mix, w_in, conv_w, a_log, dt_bias, g_dn_out, w_out, g_post_mix,
              g_pre_ffn, w_gate_up, w_down, g_post_ffn, w_ple_proj, w_ple_gate, g_ple):
    yp, ys = x_prompt, x_sample
    conv_p, delta_p, wk_p, wv_p = [], [], [], []
    conv_s, delta_s, wk_s, wv_s = [], [], [], []
    for i in range(DEPTH):
        lw = (g_pre_mix[i], w_in[i], conv_w[i], a_log[i], dt_bias[i], g_dn_out[i], w_out[i], g_post_mix[i],
              g_pre_ffn[i], w_gate_up[i], w_down[i], g_post_ffn[i], w_ple_proj[i], w_ple_gate[i], g_ple[i])
        yp, c1, d1, k1, v1 = decoder_layer(yp, p_prompt[i], None, rel_bias, *lw)
        ys, c2, d2, k2, v2 = decoder_layer(
            ys, p_sample[i], (cache_conv[i], state_delta[i], cache_win_k[i], cache_win_v[i]), rel_bias, *lw)
        conv_p.append(c1); delta_p.append(d1); wk_p.append(k1); wv_p.append(v1)
        conv_s.append(c2); delta_s.append(d2); wk_s.append(k2); wv_s.append(v2)
    return (yp, ys, jnp.stack(conv_p), jnp.stack(delta_p), jnp.stack(wk_p), jnp.stack(wv_p),
            jnp.stack(conv_s), jnp.stack(delta_s), jnp.stack(wk_s), jnp.stack(wv_s))
```

```python
import functools
import math

import numpy as np
import jax
import jax.numpy as jnp
from jax import lax
from jax.experimental import pallas as pl
from jax.experimental.pallas import tpu as pltpu

DN_HEADS = 4
DN_D = 128
CONV_W = 4
AT_HEADS = 8
AT_HD = 64
AT_PAIRS = AT_HEADS // 2
DIL_PAIRS = ((128, 1), (512, 4), (2048, 16))
WIN_MAX = 2048
BAND = 128
NUM_BUCKETS = 32
MAX_DIST = 2048
RMS_EPS = 1e-6
NEG = -1e30

DN_W = DN_HEADS * DN_D
CONV_CH = 3 * DN_W
AT_W = AT_HEADS * AT_HD
LANES = 128
SUPER = 16 * BAND

VMEM_LIMIT = 56 * 1024 * 1024

F32 = jnp.float32
BF16 = jnp.bfloat16


def _t5_bucket(n):
    max_exact = NUM_BUCKETS // 2
    n = np.asarray(n)
    large = max_exact + (np.log(np.maximum(n, 1) / max_exact) / math.log(MAX_DIST / max_exact)
                         * (NUM_BUCKETS - max_exact)).astype(np.int32)
    large = np.minimum(large, NUM_BUCKETS - 1)
    return np.where(n < max_exact, n, large).astype(np.int32)


def _sigmoid(x):
    return 1.0 / (1.0 + jnp.exp(-x))


def _silu(x):
    return x * _sigmoid(x)


def _softplus(x):
    return jnp.maximum(x, 0.0) + jnp.log1p(jnp.exp(-jnp.abs(x)))


def _rms(x, g):
    return x * lax.rsqrt(jnp.mean(x * x, axis=-1, keepdims=True) + RMS_EPS) * g


def _dot(a, b):
    return jnp.dot(a, b, preferred_element_type=F32)


def _dot_nt(a, b):
    return lax.dot_general(a, b, (((1,), (1,)), ((), ())), preferred_element_type=F32)


def _const_spec(shape):
    nd = len(shape)
    return pl.BlockSpec(shape, lambda *_: (0,) * nd, pipeline_mode=pl.Buffered(1))


C_CONV = 0
C_Z = CONV_CH
C_Q = C_Z + DN_W
C_K = C_Q + AT_W
C_V = C_K + AT_W
C_BA = C_V + AT_W
IN_COLS_PAD = C_BA + LANES


def _project(x, g, w_ref, lo, hi):
    h = _rms(x, g).astype(BF16)
    return h, _dot(h, w_ref[:, lo:hi])


def _dn_activations(c, ba_raw, ap_ref):
    c = _silu(c)
    qs, ks = [], []
    for h in range(DN_HEADS):
        qh = c[:, h * DN_D:(h + 1) * DN_D]
        kh = c[:, DN_W + h * DN_D:DN_W + (h + 1) * DN_D]
        qs.append(qh * lax.rsqrt(jnp.sum(qh * qh, axis=-1, keepdims=True) + 1e-6) * (DN_D ** -0.5))
        ks.append(kh * lax.rsqrt(jnp.sum(kh * kh, axis=-1, keepdims=True) + 1e-6))
    q = jnp.concatenate(qs, axis=1)
    k = jnp.concatenate(ks, axis=1)
    v = c[:, 2 * DN_W:]
    lane = lax.broadcasted_iota(jnp.int32, ba_raw.shape, 1)
    beta = _sigmoid(ba_raw)
    gdec = -jnp.exp(ap_ref[0:1, :]) * _softplus(ba_raw + ap_ref[1:2, :])
    ba = jnp.where(lane < DN_HEADS, beta, gdec)
    return q, k, v, ba


def _inproj_prompt_kernel(x_ref, g_ref, w_ref, cw_ref, ap_ref,
                          q_ref, k_ref, v_ref, z_ref, ba_ref, qa_ref, ka_ref, va_ref,
                          wk_ref, wv_ref, nc_ref, buf_ref, *, tm):
    i = pl.program_id(1)

    @pl.when(i == 0)
    def _():
        buf_ref[0:8, :] = jnp.zeros((8, CONV_CH), F32)

    x = x_ref[...]
    h = _rms(x, g_ref[...]).astype(BF16)
    u = _dot(h, w_ref[:, C_CONV:C_Z])
    buf_ref[8:8 + tm, :] = u
    c = u * cw_ref[CONV_W - 1:CONV_W, :]
    for t in range(CONV_W - 1):
        c = c + buf_ref[8 - (CONV_W - 1) + t:8 - (CONV_W - 1) + t + tm, :] * cw_ref[t:t + 1, :]
    buf_ref[0:8, :] = u[tm - 8:tm, :]
    nc_ref[...] = u[tm - 8:tm, :]

    ba_raw = _dot(h, w_ref[:, C_BA:C_BA + LANES])
    q, k, v, ba = _dn_activations(c, ba_raw, ap_ref)
    q_ref[...] = q.astype(BF16)
    k_ref[...] = k.astype(BF16)
    v_ref[...] = v.astype(BF16)
    ba_ref[...] = ba
    z_ref[...] = _dot(h, w_ref[:, C_Z:C_Q]).astype(BF16)

    qa = _dot(h, w_ref[:, C_Q:C_K]) * (AT_HD ** -0.5)
    ka = _dot(h, w_ref[:, C_K:C_V])
    va = _dot(h, w_ref[:, C_V:C_BA])
    wk_ref[...] = ka
    wv_ref[...] = va
    for p in range(AT_PAIRS):
        sl = slice(p * LANES, (p + 1) * LANES)
        qa_ref[p] = qa[:, sl].astype(BF16)
        ka_ref[p] = ka[:, sl].astype(BF16)
        va_ref[p] = va[:, sl].astype(BF16)


def _inproj_prompt(x, g, w, cw, ap, *, tm):
    B, S, D = x.shape
    n_keep = min(WIN_MAX, S)
    assert S % tm == 0 and n_keep % tm == 0 and tm % 8 == 0
    nt = S // tm
    first_keep = (S - n_keep) // tm
    tok = lambda width: pl.BlockSpec((None, tm, width), lambda b, i: (b, i, 0))
    pair = pl.BlockSpec((None, AT_PAIRS, tm, LANES), lambda b, i: (b, 0, i, 0))
    keep = pl.BlockSpec((None, tm, AT_W), lambda b, i: (b, jnp.maximum(i - first_keep, 0), 0))
    bf = lambda width: jax.ShapeDtypeStruct((B, S, width), BF16)
    pairs = jax.ShapeDtypeStruct((B, AT_PAIRS, S, LANES), BF16)
    return pl.pallas_call(
        functools.partial(_inproj_prompt_kernel, tm=tm),
        grid=(B, nt),
        in_specs=[tok(D), _const_spec((1, D)), _const_spec(w.shape), _const_spec(cw.shape),
                  _const_spec(ap.shape)],
        out_specs=[tok(DN_W), tok(DN_W), tok(DN_W), tok(DN_W), tok(LANES), pair, pair, pair,
                   keep, keep, pl.BlockSpec((None, 8, CONV_CH), lambda b, i: (b, 0, 0))],
        out_shape=[bf(DN_W), bf(DN_W), bf(DN_W), bf(DN_W), jax.ShapeDtypeStruct((B, S, LANES), F32),
                   pairs, pairs, pairs,
                   jax.ShapeDtypeStruct((B, n_keep, AT_W), F32), jax.ShapeDtypeStruct((B, n_keep, AT_W), F32),
                   jax.ShapeDtypeStruct((B, 8, CONV_CH), F32)],
        scratch_shapes=[pltpu.VMEM((tm + 8, CONV_CH), F32)],
        compiler_params=pltpu.CompilerParams(dimension_semantics=("arbitrary", "arbitrary"),
                                             vmem_limit_bytes=VMEM_LIMIT),
        name="inproj_prompt",
    )(x, g, w, cw, ap)


def _inproj_sample_kernel(x_ref, g_ref, w_ref, cw_ref, ap_ref, cc_ref,
                          qt_ref, kt_ref, v_ref, z_ref, ba_ref, qa_ref, ka_ref, va_ref, nc_ref):
    x = x_ref[...]
    h = _rms(x, g_ref[...]).astype(BF16)
    u = _dot(h, w_ref[:, C_CONV:C_Z])
    c = u * cw_ref[CONV_W - 1:CONV_W, :]
    for t in range(CONV_W - 1):
        c = c + cc_ref[:, t * CONV_CH:(t + 1) * CONV_CH] * cw_ref[t:t + 1, :]
    for t in range(1, CONV_W - 1):
        nc_ref[:, (t - 1) * CONV_CH:t * CONV_CH] = cc_ref[:, t * CONV_CH:(t + 1) * CONV_CH]
    nc_ref[:, (CONV_W - 2) * CONV_CH:] = u

    ba_raw = _dot(h, w_ref[:, C_BA:C_BA + LANES])
    q, k, v, ba = _dn_activations(c, ba_raw, ap_ref)
    qt_ref[...] = q.T
    kt_ref[...] = k.T
    v_ref[...] = v
    ba_ref[...] = ba
    z_ref[...] = _dot(h, w_ref[:, C_Z:C_Q])
    qa_ref[...] = _dot(h, w_ref[:, C_Q:C_K]) * (AT_HD ** -0.5)
    ka_ref[...] = _dot(h, w_ref[:, C_K:C_V])
    va_ref[...] = _dot(h, w_ref[:, C_V:C_BA])


def _inproj_sample(x, g, w, cw, ap, cache_conv2d):
    B, D = x.shape
    f = lambda *s: jax.ShapeDtypeStruct(s, F32)
    return pl.pallas_call(
        _inproj_sample_kernel,
        out_shape=[f(DN_W, B), f(DN_W, B), f(B, DN_W), f(B, DN_W), f(B, LANES),
                   f(B, AT_W), f(B, AT_W), f(B, AT_W), f(B, (CONV_W - 1) * CONV_CH)],
        compiler_params=pltpu.CompilerParams(vmem_limit_bytes=VMEM_LIMIT),
        name="inproj_sample",
    )(x, g, w, cw, ap, cache_conv2d)


DN_CHUNK = 64


def _delta_prompt_kernel(q_ref, k_ref, v_ref, z_ref, ba_ref, gn_ref, o_ref, s_out_ref, s_ref, *, tb):
    i = pl.program_id(1)
    C = DN_CHUNK

    @pl.when(i == 0)
    def _():
        s_ref[...] = jnp.zeros(s_ref.shape, F32)

    ba = ba_ref[...]
    rt = lax.broadcasted_iota(jnp.int32, (tb, tb), 0)
    ct = lax.broadcasted_iota(jnp.int32, (tb, tb), 1)
    tri = jnp.where((rt >= ct) & ((rt // C) == (ct // C)), 1.0, 0.0).astype(F32)
    gc_all = jnp.dot(tri, ba, preferred_element_type=F32, precision=lax.Precision.HIGHEST)
    gc_t = gc_all.T
    eg_all = jnp.exp(gc_all)

    ri = lax.broadcasted_iota(jnp.int32, (C, C), 0)
    ci = lax.broadcasted_iota(jnp.int32, (C, C), 1)
    eye = jnp.where(ri == ci, 1.0, 0.0).astype(F32)
    gn = gn_ref[...]

    for h in range(DN_HEADS):
        lanes = slice(h * DN_D, (h + 1) * DN_D)
        kf_all = k_ref[:, lanes].astype(F32)
        kt_all = kf_all.T
        for c in range(tb // C):
            rows = slice(c * C, (c + 1) * C)
            kf = kf_all[rows]
            qf = q_ref[rows, lanes].astype(F32)
            vf = v_ref[rows, lanes].astype(F32)
            beta = ba[rows, h:h + 1]
            gcol = gc_all[rows, DN_HEADS + h:DN_HEADS + h + 1]
            grow = gc_t[DN_HEADS + h:DN_HEADS + h + 1, rows]
            egc = eg_all[rows, DN_HEADS + h:DN_HEADS + h + 1]
            glast = gc_all[c * C + C - 1:c * C + C, DN_HEADS + h:DN_HEADS + h + 1]

            dec = jnp.exp(jnp.where(ri >= ci, gcol - grow, NEG))
            kb = kf * beta
            both = _dot_nt(jnp.concatenate([kb, qf], axis=0).astype(BF16), kf.astype(BF16))
            a_mat = jnp.where(ri > ci, both[:C] * dec, 0.0)
            intra = both[C:] * dec

            nb = (-a_mat).astype(BF16)
            xm = eye - a_mat
            pm = _dot(nb, nb)
            for _ in range(4):
                pb = pm.astype(BF16)
                r = _dot(jnp.concatenate([xm, pm], axis=0).astype(BF16), pb)
                xm = xm + r[:C]
                pm = r[C:]
            xm = xm + _dot(xm.astype(BF16), pm.astype(BF16))

            rhs = jnp.concatenate([vf * beta, kb * egc], axis=1).astype(BF16)
            uw = _dot(xm.astype(BF16), rhs)
            u = uw[:, :DN_D]
            w = uw[:, DN_D:]

            s_h = s_ref[h]
            r2 = _dot(jnp.concatenate([w, qf * egc], axis=0).astype(BF16), s_h.astype(BF16))
            v_new = u - r2[:C]
            vb = v_new.astype(BF16)
            o = r2[C:] + _dot(intra.astype(BF16), vb)
            k_dec_t = kt_all[:, rows] * jnp.exp(glast - grow)
            s_ref[h] = s_h * jnp.exp(glast) + _dot(k_dec_t.astype(BF16), vb)

            zf = z_ref[rows, lanes].astype(F32)
            o_ref[rows, lanes] = (_rms(o, gn) * _silu(zf)).astype(BF16)

    @pl.when(i == pl.num_programs(1) - 1)
    def _():
        s_out_ref[...] = s_ref[...]


def _delta_prompt(q, k, v, z, ba, gn, *, tb):
    B, S, _ = q.shape
    assert S % tb == 0 and tb % DN_CHUNK == 0
    tok = lambda width: pl.BlockSpec((None, tb, width), lambda b, i: (b, i, 0))
    st = pl.BlockSpec((None, DN_HEADS, DN_D, DN_D), lambda b, i: (b, 0, 0, 0))
    return pl.pallas_call(
        functools.partial(_delta_prompt_kernel, tb=tb),
        grid=(B, S // tb),
        in_specs=[tok(DN_W), tok(DN_W), tok(DN_W), tok(DN_W), tok(LANES), _const_spec((1, DN_D))],
        out_specs=[tok(DN_W), st],
        out_shape=[jax.ShapeDtypeStruct((B, S, DN_W), BF16),
                   jax.ShapeDtypeStruct((B, DN_HEADS, DN_D, DN_D), F32)],
        scratch_shapes=[pltpu.VMEM((DN_HEADS, DN_D, DN_D), F32)],
        compiler_params=pltpu.CompilerParams(dimension_semantics=("arbitrary", "arbitrary"),
                                             vmem_limit_bytes=VMEM_LIMIT),
        name="delta_prompt",
    )(q, k, v, z, ba, gn)


def _delta_sample_kernel(qt_ref, kt_ref, v_ref, z_ref, ba_ref, gn_ref, s_ref, o_ref, s_out_ref):
    b = pl.program_id(0)
    nb = qt_ref.shape[1]
    sel = lax.broadcasted_iota(jnp.int32, (DN_D, nb), 1) == b
    ba = ba_ref[...]
    gn = gn_ref[...]
    outs = []
    for h in range(DN_HEADS):
        rows = slice(h * DN_D, (h + 1) * DN_D)
        qcol = jnp.sum(jnp.where(sel, qt_ref[rows, :], 0.0), axis=1, keepdims=True)
        kcol = jnp.sum(jnp.where(sel, kt_ref[rows, :], 0.0), axis=1, keepdims=True)
        beta = ba[:, h:h + 1]
        gdec = ba[:, DN_HEADS + h:DN_HEADS + h + 1]
        s = s_ref[h] * jnp.exp(gdec)
        kv = jnp.sum(kcol * s, axis=0, keepdims=True)
        delta = (v_ref[:, rows] - kv) * beta
        s = s + kcol * delta
        s_out_ref[h] = s
        o = jnp.sum(qcol * s, axis=0, keepdims=True)
        outs.append(_rms(o, gn) * _silu(z_ref[:, rows]))
    o_ref[...] = jnp.concatenate(outs, axis=1)


def _delta_sample(qt, kt, v, z, ba, gn, state):
    B = v.shape[0]
    row = lambda width: pl.BlockSpec((None, 1, width), lambda b: (b, 0, 0))
    st = pl.BlockSpec((None, DN_HEADS, DN_D, DN_D), lambda b: (b, 0, 0, 0))
    return pl.pallas_call(
        _delta_sample_kernel,
        grid=(B,),
        in_specs=[_const_spec(qt.shape), _const_spec(kt.shape), row(DN_W), row(DN_W), row(LANES),
                  _const_spec((1, DN_D)), st],
        out_specs=[row(DN_W), st],
        out_shape=[jax.ShapeDtypeStruct((B, 1, DN_W), F32), jax.ShapeDtypeStruct(state.shape, F32)],
        compiler_params=pltpu.CompilerParams(dimension_semantics=("arbitrary",)),
        name="delta_sample",
    )(qt, kt, v.reshape(B, 1, DN_W), z.reshape(B, 1, DN_W), ba.reshape(B, 1, LANES), gn, state)


def _stack_position(d):
    a = np.arange(BAND)
    rows = 8 * d
    return (16 // d) * (a % rows) + a // rows


def _band_tables(rel_bias):
    tables = []
    for w, d in DIL_PAIRS:
        assert w // d == BAND
        pos = _stack_position(d)
        dist = pos[:, None] - np.concatenate([pos - BAND, pos])[None, :]
        valid = (dist >= 0) & (dist <= BAND)
        bucket = _t5_bucket(np.clip(dist, 0, BAND) * d)
        bias = jnp.where(valid[None], rel_bias[bucket].transpose(2, 0, 1).astype(F32), NEG)
        tables.append(bias.reshape(AT_PAIRS, 2 * BAND, 2 * BAND))
    return jnp.stack(tables)


def _attn_unit(q, kk, vv, table, pen):
    lane = lax.broadcasted_iota(jnp.int32, (1, LANES), 1)
    first = lane < AT_HD
    zero = jnp.zeros_like(q)
    q2 = jnp.concatenate([jnp.where(first, q, zero), jnp.where(first, zero, q)], axis=0)
    s = _dot_nt(q2, kk) + table
    if pen is not None:
        s = s + pen
    m = jnp.max(s, axis=1, keepdims=True)
    p = jnp.exp(s - m)
    l = jnp.sum(p, axis=1, keepdims=True)
    pv = _dot(p.astype(BF16), vv)
    acc = jnp.where(first, pv[:BAND], pv[BAND:])
    mt = jnp.where(first, m[:BAND], m[BAND:])
    lt = jnp.where(first, l[:BAND], l[BAND:])
    return acc, mt, lt


def _attn_prompt_kernel(q_ref, k_ref, v_ref, t_ref, o_ref,
                        k16, v16, q4, k4, v4, q1, k1, v1, acc_ref, m_ref, l_ref):
    n = pl.program_id(2)

    @pl.when(n == 0)
    def _():
        zb = jnp.zeros((BAND, LANES), BF16)
        for rho in range(16):
            k16[rho, 0:BAND, :] = zb
            v16[rho, 0:BAND, :] = zb
        for rho in range(4):
            k4[rho, 0:BAND, :] = zb
            v4[rho, 0:BAND, :] = zb
        k1[0:BAND, :] = zb
        v1[0:BAND, :] = zb

    @pl.when(n > 0)
    def _():
        for rho in range(16):
            k16[rho, 0:BAND, :] = k16[rho, BAND:2 * BAND, :]
            v16[rho, 0:BAND, :] = v16[rho, BAND:2 * BAND, :]
        for rho in range(4):
            k4[rho, 0:BAND, :] = k4[rho, 4 * BAND:5 * BAND, :]
            v4[rho, 0:BAND, :] = v4[rho, 4 * BAND:5 * BAND, :]
        k1[0:BAND, :] = k1[SUPER:SUPER + BAND, :]
        v1[0:BAND, :] = v1[SUPER:SUPER + BAND, :]

    for rho in range(16):
        cols = slice(rho * LANES, (rho + 1) * LANES)
        k16[rho, BAND:2 * BAND, :] = k_ref[:, cols]
        v16[rho, BAND:2 * BAND, :] = v_ref[:, cols]
    for rho in range(4):
        for blk in range(4):
            for j in range(4):
                src = (slice(32 * blk, 32 * blk + 32), slice((4 * j + rho) * LANES, (4 * j + rho + 1) * LANES))
                dst = slice(BAND * blk + 32 * j, BAND * blk + 32 * j + 32)
                q4[rho, dst, :] = q_ref[src]
                k4[rho, BAND + dst.start:BAND + dst.stop, :] = k_ref[src]
                v4[rho, BAND + dst.start:BAND + dst.stop, :] = v_ref[src]
    for src_ref, dst_ref, off in ((q_ref, q1, 0), (k_ref, k1, BAND), (v_ref, v1, BAND)):
        for gam in range(8):
            lo, hi = [], []
            for r in range(16):
                piece = src_ref[16 * gam:16 * gam + 16, r * LANES:(r + 1) * LANES].astype(F32)
                lo.append(piece[0:8])
                hi.append(piece[8:16])
            dst_ref[off + BAND * (2 * gam):off + BAND * (2 * gam + 1), :] = jnp.concatenate(lo, axis=0).astype(BF16)
            dst_ref[off + BAND * (2 * gam + 1):off + BAND * (2 * gam + 2), :] = jnp.concatenate(hi, axis=0).astype(BF16)

    lane2 = lax.broadcasted_iota(jnp.int32, (1, 2 * BAND), 1)
    pen = jnp.where((lane2 < BAND) & (n == 0), NEG, 0.0).astype(F32)

    t16 = t_ref[2]
    for rho in range(16):
        cols = slice(rho * LANES, (rho + 1) * LANES)
        acc, mt, lt = _attn_unit(q_ref[:, cols], k16[rho], v16[rho], t16, pen)
        acc_ref[:, cols] = acc
        m_ref[:, cols] = mt
        l_ref[:, cols] = lt

    def merge(rows, cols, acc, mt, lt, final):
        m_old = m_ref[rows, cols]
        m_new = jnp.maximum(m_old, mt)
        a = jnp.exp(m_old - m_new)
        b = jnp.exp(mt - m_new)
        acc_new = acc_ref[rows, cols] * a + acc * b
        l_new = l_ref[rows, cols] * a + lt * b
        if final:
            acc_ref[rows, cols] = acc_new / l_new
        else:
            acc_ref[rows, cols] = acc_new
            m_ref[rows, cols] = m_new
            l_ref[rows, cols] = l_new

    t4 = t_ref[1]
    for rho in range(4):
        for blk in range(4):
            acc, mt, lt = _attn_unit(q4[rho, BAND * blk:BAND * (blk + 1), :],
                                     k4[rho, BAND * blk:BAND * (blk + 2), :],
                                     v4[rho, BAND * blk:BAND * (blk + 2), :],
                                     t4, pen if blk == 0 else None)
            for j in range(4):
                sub = slice(32 * j, 32 * j + 32)
                merge(slice(32 * blk, 32 * blk + 32), slice((4 * j + rho) * LANES, (4 * j + rho + 1) * LANES),
                      acc[sub], mt[sub], lt[sub], False)

    t1 = t_ref[0]
    for blk in range(16):
        acc, mt, lt = _attn_unit(q1[BAND * blk:BAND * (blk + 1), :],
                                 k1[BAND * blk:BAND * (blk + 2), :],
                                 v1[BAND * blk:BAND * (blk + 2), :],
                                 t1, pen if blk == 0 else None)
        for r in range(16):
            sub = slice(8 * r, 8 * r + 8)
            merge(slice(8 * blk, 8 * blk + 8), slice(r * LANES, (r + 1) * LANES),
                  acc[sub], mt[sub], lt[sub], True)

    o_ref[...] = acc_ref[...].astype(BF16)


def _attn_prompt(q, k, v, tables):
    B, P, S, _ = q.shape
    assert S % SUPER == 0
    nt = S // SUPER
    view = lambda t: t.reshape(B, P, S // 16, 16 * LANES)
    tile = pl.BlockSpec((None, None, BAND, 16 * LANES), lambda b, p, n: (b, p, n, 0))
    tab = pl.BlockSpec((3, None, 2 * BAND, 2 * BAND), lambda b, p, n: (0, p, 0, 0))
    vm = lambda *s: pltpu.VMEM(s, BF16)
    out = pl.pallas_call(
        _attn_prompt_kernel,
        grid=(B, P, nt),
        in_specs=[tile, tile, tile, tab],
        out_specs=tile,
        out_shape=jax.ShapeDtypeStruct((B, P, S // 16, 16 * LANES), BF16),
        scratch_shapes=[vm(16, 2 * BAND, LANES), vm(16, 2 * BAND, LANES),
                        vm(4, 4 * BAND, LANES), vm(4, 5 * BAND, LANES), vm(4, 5 * BAND, LANES),
                        vm(SUPER, LANES), vm(SUPER + BAND, LANES), vm(SUPER + BAND, LANES),
                        pltpu.VMEM((BAND, 16 * LANES), F32), pltpu.VMEM((BAND, 16 * LANES), F32),
                        pltpu.VMEM((BAND, 16 * LANES), F32)],
        compiler_params=pltpu.CompilerParams(dimension_semantics=("arbitrary", "arbitrary", "arbitrary"),
                                             vmem_limit_bytes=VMEM_LIMIT),
        name="attn_prompt",
    )(view(q), view(k), view(v), tables)
    return out.reshape(B, P, S, LANES)


def _sample_tables(rel_bias):
    tabs = []
    for w, d in DIL_PAIRS:
        bucket = _t5_bucket((BAND - np.arange(BAND)) * d)
        tabs.append(jnp.pad(rel_bias[bucket].astype(F32), ((0, 0), (0, LANES - AT_HEADS))))
    self_bias = jnp.pad(rel_bias[0:1].astype(F32), ((0, 0), (0, LANES - AT_HEADS)))
    return jnp.stack(tabs), self_bias


def _attn_sample_kernel(q_ref, kn_ref, vn_ref, k1_ref, k4_ref, k16_ref, v1_ref, v4_ref, v16_ref,
                        t_ref, t0_ref, o_ref):
    hi = lax.Precision.HIGHEST
    gi = lax.broadcasted_iota(jnp.int32, (AT_W, LANES), 0) // AT_HD
    gj = lax.broadcasted_iota(jnp.int32, (AT_W, LANES), 1)
    gather = jnp.where(gi == gj, 1.0, 0.0).astype(F32)
    ei = lax.broadcasted_iota(jnp.int32, (LANES, AT_W), 0)
    ej = lax.broadcasted_iota(jnp.int32, (LANES, AT_W), 1) // AT_HD
    spread = jnp.where(ei == ej, 1.0, 0.0).astype(F32)

    q = q_ref[...]
    s0 = jnp.dot(q * kn_ref[...], gather, preferred_element_type=F32, precision=hi) + t0_ref[...]
    scores = []
    m = s0
    for g, kr in enumerate((k1_ref, k4_ref, k16_ref)):
        s = jnp.dot(kr[...] * q, gather, preferred_element_type=F32, precision=hi) + t_ref[g]
        scores.append(s)
        m = jnp.maximum(m, jnp.max(s, axis=0, keepdims=True))
    p0 = len(DIL_PAIRS) * jnp.exp(s0 - m)
    l = p0
    acc = jnp.dot(p0, spread, preferred_element_type=F32, precision=hi) * vn_ref[...]
    for s, vr in zip(scores, (v1_ref, v4_ref, v16_ref)):
        p = jnp.exp(s - m)
        l = l + jnp.sum(p, axis=0, keepdims=True)
        pe = jnp.dot(p, spread, preferred_element_type=F32, precision=hi)
        acc = acc + jnp.sum(pe * vr[...], axis=0, keepdims=True)
    o_ref[...] = acc / jnp.dot(l, spread, preferred_element_type=F32, precision=hi)


def _attn_sample(q, kn, vn, win_k, win_v, tabs, self_bias):
    B, W, _ = win_k.shape
    assert W == WIN_MAX
    row = pl.BlockSpec((None, 1, AT_W), lambda b: (b, 0, 0))
    specs, views = [], []
    for win in (win_k, win_v):
        for w, d in DIL_PAIRS:
            views.append(win.reshape(B, W // d, d * AT_W))
            specs.append(pl.BlockSpec((None, BAND, AT_W), lambda b, blk=W // d // BAND - 1: (b, blk, 0)))
    return pl.pallas_call(
        _attn_sample_kernel,
        grid=(B,),
        in_specs=[row, row, row] + specs + [_const_spec(tabs.shape), _const_spec(self_bias.shape)],
        out_specs=row,
        out_shape=jax.ShapeDtypeStruct((B, 1, AT_W), F32),
        compiler_params=pltpu.CompilerParams(dimension_semantics=("arbitrary",)),
        name="attn_sample",
    )(q.reshape(B, 1, AT_W), kn.reshape(B, 1, AT_W), vn.reshape(B, 1, AT_W), *views, tabs, self_bias)


def _post_kernel(x_ref, od_ref, oa_ref, p_ref, wo_ref, wgu_ref, wd_ref, wpg_ref, wpp_ref, gains_ref, y_ref):
    ffn = wd_ref.shape[0]
    mix_in = jnp.concatenate([od_ref[...]] + [oa_ref[p] for p in range(AT_PAIRS)], axis=1)
    x1 = x_ref[...] + _rms(_dot(mix_in, wo_ref[...]), gains_ref[0:1, :])
    h2 = _rms(x1, gains_ref[1:2, :]).astype(BF16)
    gt = _dot(h2, wgu_ref[:, :ffn])
    up = _dot(h2, wgu_ref[:, ffn:])
    act = (_silu(gt) * up).astype(BF16)
    x2 = x1 + _rms(_dot(act, wd_ref[...]), gains_ref[2:3, :])
    gate = _sigmoid(_dot(x2.astype(BF16), wpg_ref[...]))
    pe = _dot(p_ref[...].astype(BF16), wpp_ref[...])
    y_ref[...] = x2 + _rms(gate * pe, gains_ref[3:4, :])


def _post(x, o_dn, o_at, p, wo, wgu, wd, wpg, wpp, gains, *, tm):
    B, S, D = x.shape
    assert S % tm == 0
    tok = lambda width: pl.BlockSpec((None, tm, width), lambda b, i: (b, i, 0))
    return pl.pallas_call(
        _post_kernel,
        grid=(B, S // tm),
        in_specs=[tok(D), tok(DN_W), pl.BlockSpec((None, AT_PAIRS, tm, LANES), lambda b, i: (b, 0, i, 0)),
                  tok(p.shape[-1]),
                  _const_spec(wo.shape), _const_spec(wgu.shape), _const_spec(wd.shape),
                  _const_spec(wpg.shape), _const_spec(wpp.shape), _const_spec(gains.shape)],
        out_specs=tok(D),
        out_shape=jax.ShapeDtypeStruct((B, S, D), F32),
        compiler_params=pltpu.CompilerParams(dimension_semantics=("arbitrary", "arbitrary"),
                                             vmem_limit_bytes=VMEM_LIMIT),
        name="post",
    )(x, o_dn, o_at, p, wo, wgu, wd, wpg, wpp, gains)


def _layer_weights(w_in, conv_w, a_log, dt_bias, g_dn_out, w_out, g_post_mix, g_pre_ffn, w_gate_up, w_down,
                   g_post_ffn, w_ple_proj, w_ple_gate, g_ple):
    c0 = CONV_CH
    c1 = c0 + DN_W
    c2 = c1 + 2 * DN_HEADS
    w_ba = jnp.pad(w_in[:, c1:c2], ((0, 0), (0, LANES - 2 * DN_HEADS)))
    w = jnp.concatenate([w_in[:, :c1], w_in[:, c2:], w_ba], axis=1).astype(BF16)
    assert w.shape[1] == IN_COLS_PAD
    ap = jnp.zeros((8, LANES), F32)
    ap = ap.at[0, DN_HEADS:2 * DN_HEADS].set(a_log.astype(F32))
    ap = ap.at[1, DN_HEADS:2 * DN_HEADS].set(dt_bias.astype(F32))
    gains = jnp.zeros((8, w_out.shape[1]), F32)
    for r, g in enumerate((g_post_mix, g_pre_ffn, g_post_ffn, g_ple)):
        gains = gains.at[r].set(g.astype(F32))
    return dict(w=w, cw=conv_w.astype(F32), ap=ap, gn=g_dn_out.astype(F32)[None], wo=w_out.astype(BF16),
                wgu=w_gate_up.astype(BF16), wd=w_down.astype(BF16), wpg=w_ple_gate.astype(BF16),
                wpp=w_ple_proj.astype(BF16), gains=gains)


def _prompt_layer(x, p, g_pre, lw, tables):
    B, S, D = x.shape
    q, k, v, z, ba, qa, ka, va, wk, wv, nc = _inproj_prompt(x, g_pre, lw["w"], lw["cw"], lw["ap"], tm=512)
    o_dn, s_fin = _delta_prompt(q, k, v, z, ba, lw["gn"], tb=256)
    o_at = _attn_prompt(qa, ka, va, tables)
    y = _post(x, o_dn, o_at, p, lw["wo"], lw["wgu"], lw["wd"], lw["wpg"], lw["wpp"], lw["gains"], tm=256)
    n_keep = wk.shape[1]
    return (y, nc[:, 8 - (CONV_W - 1):], s_fin, wk.reshape(B, n_keep, AT_HEADS, AT_HD),
            wv.reshape(B, n_keep, AT_HEADS, AT_HD))


def _sample_layer(x, p, cache_conv, state, win_k, win_v, g_pre, lw, stabs, self_bias):
    B, T, D = x.shape
    assert T == 1
    W = win_k.shape[1]
    x2 = x.reshape(B, D)
    qt, kt, v, z, ba, qa, ka, va, nc = _inproj_sample(
        x2, g_pre, lw["w"], lw["cw"], lw["ap"], cache_conv.reshape(B, (CONV_W - 1) * CONV_CH))
    o_dn, s_new = _delta_sample(qt, kt, v, z, ba, lw["gn"], state)
    o_at = _attn_sample(qa, ka, va, win_k.reshape(B, W, AT_W), win_v.reshape(B, W, AT_W), stabs, self_bias)
    o_at = o_at.reshape(1, B, AT_PAIRS, LANES).transpose(0, 2, 1, 3).astype(BF16)
    y = _post(x2[None], o_dn.reshape(1, B, DN_W).astype(BF16), o_at, p.reshape(1, B, -1), lw["wo"], lw["wgu"],
              lw["wd"], lw["wpg"], lw["wpp"], lw["gains"], tm=B)
    new_wk = jnp.concatenate([win_k[:, 1:], ka.reshape(B, 1, AT_HEADS, AT_HD)], axis=1)
    new_wv = jnp.concatenate([win_v[:, 1:], va.reshape(B, 1, AT_HEADS, AT_HD)], axis=1)
    return y.reshape(B, 1, D), nc.reshape(B, CONV_W - 1, CONV_CH), s_new, new_wk, new_wv


def kernel(x_prompt, x_sample, cache_conv, state_delta, cache_win_k, cache_win_v, p_prompt, p_sample, rel_bias, g_pre_mix, w_in, conv_w, a_log, dt_bias, g_dn_out, w_out, g_post_mix, g_pre_ffn, w_gate_up, w_down, g_post_ffn, w_ple_proj, w_ple_gate, g_ple):
    depth = w_in.shape[0]
    tables = _band_tables(rel_bias)
    stabs, self_bias = _sample_tables(rel_bias)
    yp, ys = x_prompt, x_sample
    outs = [[] for _ in range(8)]
    for i in range(depth):
        lw = _layer_weights(w_in[i], conv_w[i], a_log[i], dt_bias[i], g_dn_out[i], w_out[i], g_post_mix[i],
                            g_pre_ffn[i], w_gate_up[i], w_down[i], g_post_ffn[i], w_ple_proj[i], w_ple_gate[i],
                            g_ple[i])
        g_pre = g_pre_mix[i].astype(F32)[None]
        yp, c1, d1, k1, v1 = _prompt_layer(yp, p_prompt[i], g_pre, lw, tables)
        ys, c2, d2, k2, v2 = _sample_layer(ys, p_sample[i], cache_conv[i], state_delta[i], cache_win_k[i],
                                           cache_win_v[i], g_pre, lw, stabs, self_bias)
        for lst, val in zip(outs, (c1, d1, k1, v1, c2, d2, k2, v2)):
            lst.append(val)
    return (yp, ys) + tuple(jnp.stack(o) for o in outs)
```

```python
import functools
import math

import numpy as np
import jax
import jax.numpy as jnp
from jax import lax
from jax.experimental import pallas as pl
from jax.experimental.pallas import tpu as pltpu

DN_HEADS = 4
DN_D = 128
CONV_W = 4
AT_HEADS = 8
AT_HD = 64
AT_PAIRS = AT_HEADS // 2
DIL_PAIRS = ((128, 1), (512, 4), (2048, 16))
WIN_MAX = 2048
BAND = 128
NUM_BUCKETS = 32
MAX_DIST = 2048
RMS_EPS = 1e-6
NEG = -1e30

DN_W = DN_HEADS * DN_D
CONV_CH = 3 * DN_W
AT_W = AT_HEADS * AT_HD
LANES = 128
SUPER = 16 * BAND

VMEM_LIMIT = 56 * 1024 * 1024

F32 = jnp.float32
BF16 = jnp.bfloat16


def _t5_bucket(n):
    max_exact = NUM_BUCKETS // 2
    n = np.asarray(n)
    large = max_exact + (np.log(np.maximum(n, 1) / max_exact) / math.log(MAX_DIST / max_exact)
                         * (NUM_BUCKETS - max_exact)).astype(np.int32)
    large = np.minimum(large, NUM_BUCKETS - 1)
    return np.where(n < max_exact, n, large).astype(np.int32)


def _sigmoid(x):
    return 1.0 / (1.0 + jnp.exp(-x))


def _silu(x):
    return x * _sigmoid(x)


def _softplus(x):
    return jnp.maximum(x, 0.0) + jnp.log1p(jnp.exp(-jnp.abs(x)))


def _rms(x, g):
    return x * lax.rsqrt(jnp.mean(x * x, axis=-1, keepdims=True) + RMS_EPS) * g


def _dot(a, b):
    return jnp.dot(a, b, preferred_element_type=F32)


def _dot_nt(a, b):
    return lax.dot_general(a, b, (((1,), (1,)), ((), ())), preferred_element_type=F32)


def _const_spec(shape):
    nd = len(shape)
    return pl.BlockSpec(shape, lambda *_: (0,) * nd, pipeline_mode=pl.Buffered(1))


C_CONV = 0
C_Z = CONV_CH
C_Q = C_Z + DN_W
C_K = C_Q + AT_W
C_V = C_K + AT_W
C_BA = C_V + AT_W
IN_COLS_PAD = C_BA + LANES


def _dn_activations(c, ba_raw, ap_ref):
    c = _silu(c)
    qs, ks = [], []
    for h in range(DN_HEADS):
        qh = c[:, h * DN_D:(h + 1) * DN_D]
        kh = c[:, DN_W + h * DN_D:DN_W + (h + 1) * DN_D]
        qs.append(qh * lax.rsqrt(jnp.sum(qh * qh, axis=-1, keepdims=True) + 1e-6) * (DN_D ** -0.5))
        ks.append(kh * lax.rsqrt(jnp.sum(kh * kh, axis=-1, keepdims=True) + 1e-6))
    q = jnp.concatenate(qs, axis=1)
    k = jnp.concatenate(ks, axis=1)
    v = c[:, 2 * DN_W:]
    lane = lax.broadcasted_iota(jnp.int32, ba_raw.shape, 1)
    beta = _sigmoid(ba_raw)
    gdec = -jnp.exp(ap_ref[0:1, :]) * _softplus(ba_raw + ap_ref[1:2, :])
    ba = jnp.where(lane < DN_HEADS, beta, gdec)
    return q, k, v, ba


def _inproj_prompt_kernel(x_ref, g_ref, w_ref, cw_ref, ap_ref,
                          q_ref, k_ref, v_ref, z_ref, ba_ref, qa_ref, ka_ref, va_ref,
                          wk_ref, wv_ref, nc_ref, buf_ref, *, tm, first_keep):
    i = pl.program_id(1)

    @pl.when(i == 0)
    def _():
        buf_ref[0:8, :] = jnp.zeros((8, CONV_CH), F32)

    x = x_ref[...]
    h = _rms(x, g_ref[...]).astype(BF16)
    u = _dot(h, w_ref[:, C_CONV:C_Z])
    buf_ref[8:8 + tm, :] = u
    c = u * cw_ref[CONV_W - 1:CONV_W, :]
    for t in range(CONV_W - 1):
        c = c + buf_ref[8 - (CONV_W - 1) + t:8 - (CONV_W - 1) + t + tm, :] * cw_ref[t:t + 1, :]
    buf_ref[0:8, :] = u[tm - 8:tm, :]
    nc_ref[...] = u[tm - 8:tm, :]

    ba_raw = _dot(h, w_ref[:, C_BA:C_BA + LANES])
    q, k, v, ba = _dn_activations(c, ba_raw, ap_ref)
    q_ref[...] = q.astype(BF16)
    k_ref[...] = k.astype(BF16)
    v_ref[...] = v.astype(BF16)
    ba_ref[...] = ba
    z_ref[...] = _dot(h, w_ref[:, C_Z:C_Q]).astype(BF16)

    qa = _dot(h, w_ref[:, C_Q:C_K]) * (AT_HD ** -0.5)
    ka = _dot(h, w_ref[:, C_K:C_V])
    va = _dot(h, w_ref[:, C_V:C_BA])

    @pl.when(i >= first_keep)
    def _():
        wk_ref[...] = ka.T
        wv_ref[...] = va.T

    for p in range(AT_PAIRS):
        sl = slice(p * LANES, (p + 1) * LANES)
        qa_ref[p] = qa[:, sl].astype(BF16)
        ka_ref[p] = ka[:, sl].astype(BF16)
        va_ref[p] = va[:, sl].astype(BF16)


def _inproj_prompt(x, g, w, cw, ap, *, tm):
    B, S, D = x.shape
    n_keep = min(WIN_MAX, S)
    assert S % tm == 0 and n_keep % tm == 0 and tm % 8 == 0
    nt = S // tm
    first_keep = (S - n_keep) // tm
    tok = lambda width: pl.BlockSpec((None, tm, width), lambda b, i: (b, i, 0))
    pair = pl.BlockSpec((None, AT_PAIRS, tm, LANES), lambda b, i: (b, 0, i, 0))
    keep = pl.BlockSpec((None, AT_W, tm), lambda b, i: (b, 0, jnp.maximum(i - first_keep, 0)))
    bf = lambda width: jax.ShapeDtypeStruct((B, S, width), BF16)
    pairs = jax.ShapeDtypeStruct((B, AT_PAIRS, S, LANES), BF16)
    return pl.pallas_call(
        functools.partial(_inproj_prompt_kernel, tm=tm, first_keep=first_keep),
        grid=(B, nt),
        in_specs=[tok(D), _const_spec((1, D)), _const_spec(w.shape), _const_spec(cw.shape),
                  _const_spec(ap.shape)],
        out_specs=[tok(DN_W), tok(DN_W), tok(DN_W), tok(DN_W), tok(LANES), pair, pair, pair,
                   keep, keep, pl.BlockSpec((None, 8, CONV_CH), lambda b, i: (b, 0, 0))],
        out_shape=[bf(DN_W), bf(DN_W), bf(DN_W), bf(DN_W), jax.ShapeDtypeStruct((B, S, LANES), F32),
                   pairs, pairs, pairs,
                   jax.ShapeDtypeStruct((B, AT_W, n_keep), F32), jax.ShapeDtypeStruct((B, AT_W, n_keep), F32),
                   jax.ShapeDtypeStruct((B, 8, CONV_CH), F32)],
        scratch_shapes=[pltpu.VMEM((tm + 8, CONV_CH), F32)],
        compiler_params=pltpu.CompilerParams(dimension_semantics=("arbitrary", "arbitrary"),
                                             vmem_limit_bytes=VMEM_LIMIT),
        name="inproj_prompt",
    )(x, g, w, cw, ap)


def _inproj_sample_kernel(x_ref, g_ref, w_ref, cw_ref, ap_ref, cc_ref,
                          qt_ref, kt_ref, v_ref, z_ref, ba_ref, qa_ref, ka_ref, va_ref, nc_ref):
    x = x_ref[...]
    h = _rms(x, g_ref[...]).astype(BF16)
    u = _dot(h, w_ref[:, C_CONV:C_Z])
    c = u * cw_ref[CONV_W - 1:CONV_W, :]
    for t in range(CONV_W - 1):
        c = c + cc_ref[t] * cw_ref[t:t + 1, :]
    for t in range(1, CONV_W - 1):
        nc_ref[t - 1] = cc_ref[t]
    nc_ref[CONV_W - 2] = u

    ba_raw = _dot(h, w_ref[:, C_BA:C_BA + LANES])
    q, k, v, ba = _dn_activations(c, ba_raw, ap_ref)
    qt_ref[...] = q.T
    kt_ref[...] = k.T
    v_ref[...] = v
    ba_ref[...] = ba
    z_ref[...] = _dot(h, w_ref[:, C_Z:C_Q])
    qa_ref[...] = _dot(h, w_ref[:, C_Q:C_K]) * (AT_HD ** -0.5)
    ka_ref[...] = _dot(h, w_ref[:, C_K:C_V])
    va_ref[...] = _dot(h, w_ref[:, C_V:C_BA])


def _inproj_sample(x, g, w, cw, ap, conv_ctx):
    B, D = x.shape
    f = lambda *s: jax.ShapeDtypeStruct(s, F32)
    return pl.pallas_call(
        _inproj_sample_kernel,
        out_shape=[f(DN_W, B), f(DN_W, B), f(B, DN_W), f(B, DN_W), f(B, LANES),
                   f(B, AT_W), f(B, AT_W), f(B, AT_W), f(CONV_W - 1, B, CONV_CH)],
        compiler_params=pltpu.CompilerParams(vmem_limit_bytes=VMEM_LIMIT),
        name="inproj_sample",
    )(x, g, w, cw, ap, conv_ctx)


DN_CHUNK = 64


def _delta_prompt_kernel(q_ref, k_ref, v_ref, z_ref, ba_ref, gn_ref, o_ref, s_out_ref, s_ref, *, nb, tb):
    i = pl.program_id(0)
    C = DN_CHUNK
    nc = tb // C

    @pl.when(i == 0)
    def _():
        s_ref[...] = jnp.zeros(s_ref.shape, F32)

    rt = lax.broadcasted_iota(jnp.int32, (tb, tb), 0)
    ct = lax.broadcasted_iota(jnp.int32, (tb, tb), 1)
    tri = jnp.where((rt >= ct) & ((rt // C) == (ct // C)), 1.0, 0.0).astype(F32)
    ri = lax.broadcasted_iota(jnp.int32, (C, C), 0)
    ci = lax.broadcasted_iota(jnp.int32, (C, C), 1)
    eye = jnp.where(ri == ci, 1.0, 0.0).astype(F32)
    gn = gn_ref[...]

    units = [(b, h, c) for b in range(nb) for h in range(DN_HEADS) for c in range(nc)]
    pre = {}
    for b in range(nb):
        ba = ba_ref[b]
        gc_all = jnp.dot(tri, ba, preferred_element_type=F32, precision=lax.Precision.HIGHEST)
        pre[b] = (ba, gc_all, gc_all.T, jnp.exp(gc_all))
    kts = {(b, h): k_ref[b, :, h * DN_D:(h + 1) * DN_D].astype(F32).T
           for b in range(nb) for h in range(DN_HEADS)}

    st = {}
    for (b, h, c) in units:
        ba, gc_all, gc_t, eg_all = pre[b]
        rows = slice(c * C, (c + 1) * C)
        lanes = slice(h * DN_D, (h + 1) * DN_D)
        gl = DN_HEADS + h
        kf = k_ref[b, rows, lanes].astype(F32)
        qf = q_ref[b, rows, lanes].astype(F32)
        vf = v_ref[b, rows, lanes].astype(F32)
        beta = ba[rows, h:h + 1]
        gcol = gc_all[rows, gl:gl + 1]
        grow = gc_t[gl:gl + 1, rows]
        egc = eg_all[rows, gl:gl + 1]
        glast = gc_all[c * C + C - 1:c * C + C, gl:gl + 1]
        kb = kf * beta
        st[b, h, c] = dict(
            dec=jnp.exp(jnp.where(ri >= ci, gcol - grow, NEG)),
            lhs=jnp.concatenate([kb, qf], axis=0).astype(BF16),
            kbf=kf.astype(BF16),
            rhs=jnp.concatenate([vf * beta, kb * egc], axis=1).astype(BF16),
            qdec=(qf * egc).astype(BF16),
            kdt=(kts[b, h][:, rows] * jnp.exp(glast - grow)).astype(BF16),
            glast=jnp.exp(glast))
    for u in units:
        st[u]["both"] = _dot_nt(st[u]["lhs"], st[u]["kbf"])
    for u in units:
        d = st[u]
        a_mat = jnp.where(ri > ci, d["both"][:C] * d["dec"], 0.0)
        d["intra"] = (d["both"][C:] * d["dec"]).astype(BF16)
        d["nb"] = (-a_mat).astype(BF16)
        d["xm"] = eye - a_mat
    for u in units:
        st[u]["pm"] = _dot(st[u]["nb"], st[u]["nb"])
    for _ in range(4):
        for u in units:
            d = st[u]
            d["r"] = _dot(jnp.concatenate([d["xm"], d["pm"]], axis=0).astype(BF16), d["pm"].astype(BF16))
        for u in units:
            d = st[u]
            d["xm"] = d["xm"] + d["r"][:C]
            d["pm"] = d["r"][C:]
    for u in units:
        d = st[u]
        d["r"] = _dot(d["xm"].astype(BF16), d["pm"].astype(BF16))
    for u in units:
        d = st[u]
        d["uw"] = _dot((d["xm"] + d["r"]).astype(BF16), d["rhs"])

    chains = [(b, h) for b in range(nb) for h in range(DN_HEADS)]
    state = {bh: s_ref[bh[0], bh[1]] for bh in chains}
    for c in range(nc):
        rows = slice(c * C, (c + 1) * C)
        r2 = {}
        for (b, h) in chains:
            d = st[b, h, c]
            lhs = jnp.concatenate([d["uw"][:, DN_D:].astype(BF16), d["qdec"]], axis=0)
            r2[b, h] = _dot(lhs, state[b, h].astype(BF16))
        vb = {bh: (st[bh[0], bh[1], c]["uw"][:, :DN_D] - r2[bh][:C]).astype(BF16) for bh in chains}
        for (b, h) in chains:
            d = st[b, h, c]
            o = r2[b, h][C:] + _dot(d["intra"], vb[b, h])
            state[b, h] = state[b, h] * d["glast"] + _dot(d["kdt"], vb[b, h])
            lanes = slice(h * DN_D, (h + 1) * DN_D)
            zf = z_ref[b, rows, lanes].astype(F32)
            o_ref[b, rows, lanes] = (_rms(o, gn) * _silu(zf)).astype(BF16)
    for (b, h) in chains:
        s_ref[b, h] = state[b, h]

    @pl.when(i == pl.num_programs(0) - 1)
    def _():
        s_out_ref[...] = s_ref[...]


def _delta_prompt(q, k, v, z, ba, gn, *, tb):
    B, S, _ = q.shape
    assert S % tb == 0 and tb % DN_CHUNK == 0
    tok = lambda width: pl.BlockSpec((B, tb, width), lambda i: (0, i, 0))
    st = pl.BlockSpec((B, DN_HEADS, DN_D, DN_D), lambda i: (0, 0, 0, 0))
    return pl.pallas_call(
        functools.partial(_delta_prompt_kernel, nb=B, tb=tb),
        grid=(S // tb,),
        in_specs=[tok(DN_W), tok(DN_W), tok(DN_W), tok(DN_W), tok(LANES), _const_spec((1, DN_D))],
        out_specs=[tok(DN_W), st],
        out_shape=[jax.ShapeDtypeStruct((B, S, DN_W), BF16),
                   jax.ShapeDtypeStruct((B, DN_HEADS, DN_D, DN_D), F32)],
        scratch_shapes=[pltpu.VMEM((B, DN_HEADS, DN_D, DN_D), F32)],
        compiler_params=pltpu.CompilerParams(dimension_semantics=("arbitrary",),
                                             vmem_limit_bytes=VMEM_LIMIT),
        name="delta_prompt",
    )(q, k, v, z, ba, gn)


def _delta_sample_kernel(qt_ref, kt_ref, v_ref, z_ref, ba_ref, gn_ref, s_ref, o_ref, s_out_ref):
    b = pl.program_id(0)
    nb = qt_ref.shape[1]
    sel = lax.broadcasted_iota(jnp.int32, (DN_D, nb), 1) == b
    ba = ba_ref[...]
    gn = gn_ref[...]
    outs = []
    for h in range(DN_HEADS):
        rows = slice(h * DN_D, (h + 1) * DN_D)
        qcol = jnp.sum(jnp.where(sel, qt_ref[rows, :], 0.0), axis=1, keepdims=True)
        kcol = jnp.sum(jnp.where(sel, kt_ref[rows, :], 0.0), axis=1, keepdims=True)
        beta = ba[:, h:h + 1]
        gdec = ba[:, DN_HEADS + h:DN_HEADS + h + 1]
        s = s_ref[h] * jnp.exp(gdec)
        kv = jnp.sum(kcol * s, axis=0, keepdims=True)
        delta = (v_ref[:, rows] - kv) * beta
        s = s + kcol * delta
        s_out_ref[h] = s
        o = jnp.sum(qcol * s, axis=0, keepdims=True)
        outs.append(_rms(o, gn) * _silu(z_ref[:, rows]))
    o_ref[...] = jnp.concatenate(outs, axis=1)


def _delta_sample(qt, kt, v, z, ba, gn, state):
    B = v.shape[0]
    row = lambda width: pl.BlockSpec((None, 1, width), lambda b: (b, 0, 0))
    st = pl.BlockSpec((None, DN_HEADS, DN_D, DN_D), lambda b: (b, 0, 0, 0))
    return pl.pallas_call(
        _delta_sample_kernel,
        grid=(B,),
        in_specs=[_const_spec(qt.shape), _const_spec(kt.shape), row(DN_W), row(DN_W), row(LANES),
                  _const_spec((1, DN_D)), st],
        out_specs=[row(DN_W), st],
        out_shape=[jax.ShapeDtypeStruct((B, 1, DN_W), F32), jax.ShapeDtypeStruct(state.shape, F32)],
        compiler_params=pltpu.CompilerParams(dimension_semantics=("arbitrary",)),
        name="delta_sample",
    )(qt, kt, v.reshape(B, 1, DN_W), z.reshape(B, 1, DN_W), ba.reshape(B, 1, LANES), gn, state)


def _stack_position(d):
    a = np.arange(BAND)
    rows = 8 * d
    return (16 // d) * (a % rows) + a // rows


def _bias_lookup(rel_bias, bucket):
    onehot = jnp.asarray(bucket[..., None] == np.arange(NUM_BUCKETS), dtype=BF16).astype(F32)
    return jnp.einsum("...n,nh->h...", onehot, rel_bias.astype(F32), precision=lax.Precision.HIGHEST)


def _band_tables(rel_bias):
    tables = []
    for w, d in DIL_PAIRS:
        assert w // d == BAND
        pos = _stack_position(d)
        dist = pos[:, None] - np.concatenate([pos - BAND, pos])[None, :]
        valid = (dist >= 0) & (dist <= BAND)
        bucket = _t5_bucket(np.clip(dist, 0, BAND) * d)
        bias = jnp.where(valid[None], _bias_lookup(rel_bias, bucket), NEG)
        tables.append(bias.reshape(AT_PAIRS, 2 * BAND, 2 * BAND))
    return jnp.stack(tables)


def _attn_unit(q, kk, vv, table, pen):
    lane = lax.broadcasted_iota(jnp.int32, (1, LANES), 1)
    first = lane < AT_HD
    zero = jnp.zeros_like(q)
    q2 = jnp.concatenate([jnp.where(first, q, zero), jnp.where(first, zero, q)], axis=0)
    s = _dot_nt(q2, kk) + table
    if pen is not None:
        s = s + pen
    m = jnp.max(s, axis=1, keepdims=True)
    p = jnp.exp(s - m)
    l = jnp.sum(p, axis=1, keepdims=True)
    pv = _dot(p.astype(BF16), vv)
    acc = jnp.where(first, pv[:BAND], pv[BAND:])
    mt = jnp.where(first, m[:BAND], m[BAND:])
    lt = jnp.where(first, l[:BAND], l[BAND:])
    return acc, mt, lt


def _attn_prompt_kernel(q_ref, k_ref, v_ref, t_ref, o_ref,
                        k16, v16, q4, k4, v4, q1, k1, v1, acc_ref, m_ref, l_ref):
    n = pl.program_id(2)

    @pl.when(n == 0)
    def _():
        zb = jnp.zeros((BAND, LANES), BF16)
        for rho in range(16):
            k16[rho, 0:BAND, :] = zb
            v16[rho, 0:BAND, :] = zb
        for rho in range(4):
            k4[rho, 0:BAND, :] = zb
            v4[rho, 0:BAND, :] = zb
        k1[0:BAND, :] = zb
        v1[0:BAND, :] = zb

    @pl.when(n > 0)
    def _():
        for rho in range(16):
            k16[rho, 0:BAND, :] = k16[rho, BAND:2 * BAND, :]
            v16[rho, 0:BAND, :] = v16[rho, BAND:2 * BAND, :]
        for rho in range(4):
            k4[rho, 0:BAND, :] = k4[rho, 4 * BAND:5 * BAND, :]
            v4[rho, 0:BAND, :] = v4[rho, 4 * BAND:5 * BAND, :]
        k1[0:BAND, :] = k1[SUPER:SUPER + BAND, :]
        v1[0:BAND, :] = v1[SUPER:SUPER + BAND, :]

    for rho in range(16):
        cols = slice(rho * LANES, (rho + 1) * LANES)
        k16[rho, BAND:2 * BAND, :] = k_ref[:, cols]
        v16[rho, BAND:2 * BAND, :] = v_ref[:, cols]
    for rho in range(4):
        for blk in range(4):
            for j in range(4):
                src = (slice(32 * blk, 32 * blk + 32), slice((4 * j + rho) * LANES, (4 * j + rho + 1) * LANES))
                dst = slice(BAND * blk + 32 * j, BAND * blk + 32 * j + 32)
                q4[rho, dst, :] = q_ref[src]
                k4[rho, BAND + dst.start:BAND + dst.stop, :] = k_ref[src]
                v4[rho, BAND + dst.start:BAND + dst.stop, :] = v_ref[src]
    for src_ref, dst_ref, off in ((q_ref, q1, 0), (k_ref, k1, BAND), (v_ref, v1, BAND)):
        for gam in range(8):
            lo, hi = [], []
            for r in range(16):
                piece = src_ref[16 * gam:16 * gam + 16, r * LANES:(r + 1) * LANES].astype(F32)
                lo.append(piece[0:8])
                hi.append(piece[8:16])
            dst_ref[off + BAND * (2 * gam):off + BAND * (2 * gam + 1), :] = jnp.concatenate(lo, axis=0).astype(BF16)
            dst_ref[off + BAND * (2 * gam + 1):off + BAND * (2 * gam + 2), :] = jnp.concatenate(hi, axis=0).astype(BF16)

    lane2 = lax.broadcasted_iota(jnp.int32, (1, 2 * BAND), 1)
    pen = jnp.where((lane2 < BAND) & (n == 0), NEG, 0.0).astype(F32)

    t16 = t_ref[2]
    for rho in range(16):
        cols = slice(rho * LANES, (rho + 1) * LANES)
        acc, mt, lt = _attn_unit(q_ref[:, cols], k16[rho], v16[rho], t16, pen)
        acc_ref[:, cols] = acc
        m_ref[:, cols] = mt
        l_ref[:, cols] = lt

    def merge(rows, cols, acc, mt, lt, final):
        m_old = m_ref[rows, cols]
        m_new = jnp.maximum(m_old, mt)
        a = jnp.exp(m_old - m_new)
        b = jnp.exp(mt - m_new)
        acc_new = acc_ref[rows, cols] * a + acc * b
        l_new = l_ref[rows, cols] * a + lt * b
        if final:
            acc_ref[rows, cols] = acc_new / l_new
        else:
            acc_ref[rows, cols] = acc_new
            m_ref[rows, cols] = m_new
            l_ref[rows, cols] = l_new

    t4 = t_ref[1]
    for rho in range(4):
        for blk in range(4):
            acc, mt, lt = _attn_unit(q4[rho, BAND * blk:BAND * (blk + 1), :],
                                     k4[rho, BAND * blk:BAND * (blk + 2), :],
                                     v4[rho, BAND * blk:BAND * (blk + 2), :],
                                     t4, pen if blk == 0 else None)
            for j in range(4):
                sub = slice(32 * j, 32 * j + 32)
                merge(slice(32 * blk, 32 * blk + 32), slice((4 * j + rho) * LANES, (4 * j + rho + 1) * LANES),
                      acc[sub], mt[sub], lt[sub], False)

    t1 = t_ref[0]
    for blk in range(16):
        acc, mt, lt = _attn_unit(q1[BAND * blk:BAND * (blk + 1), :],
                                 k1[BAND * blk:BAND * (blk + 2), :],
                                 v1[BAND * blk:BAND * (blk + 2), :],
                                 t1, pen if blk == 0 else None)
        for r in range(16):
            sub = slice(8 * r, 8 * r + 8)
            merge(slice(8 * blk, 8 * blk + 8), slice(r * LANES, (r + 1) * LANES),
                  acc[sub], mt[sub], lt[sub], True)

    o_ref[...] = acc_ref[...].astype(BF16)


def _attn_prompt(q, k, v, tables):
    B, P, S, _ = q.shape
    assert S % SUPER == 0
    nt = S // SUPER
    view = lambda t: t.reshape(B, P, S // 16, 16 * LANES)
    tile = pl.BlockSpec((None, None, BAND, 16 * LANES), lambda b, p, n: (b, p, n, 0))
    tab = pl.BlockSpec((3, None, 2 * BAND, 2 * BAND), lambda b, p, n: (0, p, 0, 0))
    vm = lambda *s: pltpu.VMEM(s, BF16)
    out = pl.pallas_call(
        _attn_prompt_kernel,
        grid=(B, P, nt),
        in_specs=[tile, tile, tile, tab],
        out_specs=tile,
        out_shape=jax.ShapeDtypeStruct((B, P, S // 16, 16 * LANES), BF16),
        scratch_shapes=[vm(16, 2 * BAND, LANES), vm(16, 2 * BAND, LANES),
                        vm(4, 4 * BAND, LANES), vm(4, 5 * BAND, LANES), vm(4, 5 * BAND, LANES),
                        vm(SUPER, LANES), vm(SUPER + BAND, LANES), vm(SUPER + BAND, LANES),
                        pltpu.VMEM((BAND, 16 * LANES), F32), pltpu.VMEM((BAND, 16 * LANES), F32),
                        pltpu.VMEM((BAND, 16 * LANES), F32)],
        compiler_params=pltpu.CompilerParams(dimension_semantics=("arbitrary", "arbitrary", "arbitrary"),
                                             vmem_limit_bytes=VMEM_LIMIT),
        name="attn_prompt",
    )(view(q), view(k), view(v), tables)
    return out.reshape(B, P, S, LANES)


def _sample_tables(rel_bias):
    W = WIN_MAX
    j = W - np.arange(W)
    count = sum(((j % d == 0) & (j <= w)).astype(np.int64) for w, d in DIL_PAIRS)
    bias = _bias_lookup(rel_bias, _t5_bucket(j))
    tab = jnp.where(count[None] > 0, bias + np.log(np.maximum(count, 1))[None].astype(np.float32), NEG)
    self_tab = rel_bias[0].astype(F32) + math.log(len(DIL_PAIRS))
    return tab, jnp.broadcast_to(self_tab[:, None], (AT_HEADS, LANES))


SAMPLE_HEADS_PER_STEP = 4


def _attn_sample_kernel(q_ref, kn_ref, vn_ref, k_ref, v_ref, t_ref, t0_ref, o_ref, ok_ref, ov_ref):
    hb, E, W = k_ref.shape
    lane = lax.broadcasted_iota(jnp.int32, (1, W), 1)
    last = lane == W - 1
    for i in range(hb):
        kh = k_ref[i]
        vh = v_ref[i]
        qc = q_ref[:, i:i + 1]
        knc = kn_ref[:, i:i + 1]
        vnc = vn_ref[:, i:i + 1]
        s = jnp.sum(kh * qc, axis=0, keepdims=True) + t_ref[i:i + 1, :]
        s0 = jnp.sum(qc * knc, axis=0, keepdims=True) + t0_ref[i:i + 1, 0:1]
        m = jnp.maximum(jnp.max(s, axis=1, keepdims=True), s0)
        p = jnp.exp(s - m)
        p0 = jnp.exp(s0 - m)
        l = jnp.sum(p, axis=1, keepdims=True) + p0
        acc = jnp.sum(vh * p, axis=1, keepdims=True) + p0 * vnc
        o_ref[:, i:i + 1] = acc / l
        ok_ref[i] = jnp.where(last, knc, pltpu.roll(kh, W - 1, axis=1))
        ov_ref[i] = jnp.where(last, vnc, pltpu.roll(vh, W - 1, axis=1))


def _attn_sample(q, kn, vn, win_k, win_v, tab, self_tab):
    B, H, E, W = win_k.shape
    assert W == WIN_MAX
    hb = SAMPLE_HEADS_PER_STEP
    cols = lambda t: t.reshape(B, H // hb, hb, E).transpose(0, 1, 3, 2)
    col = pl.BlockSpec((None, None, E, hb), lambda b, g: (b, g, 0, 0))
    slab = pl.BlockSpec((None, hb, E, W), lambda b, g: (b, g, 0, 0))
    o, new_k, new_v = pl.pallas_call(
        _attn_sample_kernel,
        grid=(B, H // hb),
        in_specs=[col, col, col, slab, slab,
                  pl.BlockSpec((None, hb, W), lambda b, g: (g, 0, 0)),
                  pl.BlockSpec((None, hb, LANES), lambda b, g: (g, 0, 0))],
        out_specs=[col, slab, slab],
        out_shape=[jax.ShapeDtypeStruct((B, H // hb, E, hb), F32),
                   jax.ShapeDtypeStruct(win_k.shape, F32), jax.ShapeDtypeStruct(win_v.shape, F32)],
        compiler_params=pltpu.CompilerParams(dimension_semantics=("arbitrary", "arbitrary"),
                                             vmem_limit_bytes=VMEM_LIMIT),
        name="attn_sample",
    )(cols(q), cols(kn), cols(vn), win_k, win_v, tab.reshape(H // hb, hb, W), self_tab.reshape(H // hb, hb, LANES))
    return o.transpose(0, 1, 3, 2).reshape(B, H * E), new_k, new_v


def _post_kernel(x_ref, od_ref, oa_ref, p_ref, wo_ref, wgu_ref, wd_ref, wpg_ref, wpp_ref, gains_ref, y_ref):
    ffn = wd_ref.shape[0]
    mix_in = jnp.concatenate([od_ref[...]] + [oa_ref[p] for p in range(AT_PAIRS)], axis=1)
    x1 = x_ref[...] + _rms(_dot(mix_in, wo_ref[...]), gains_ref[0:1, :])
    h2 = _rms(x1, gains_ref[1:2, :]).astype(BF16)
    gt = _dot(h2, wgu_ref[:, :ffn])
    up = _dot(h2, wgu_ref[:, ffn:])
    act = (_silu(gt) * up).astype(BF16)
    x2 = x1 + _rms(_dot(act, wd_ref[...]), gains_ref[2:3, :])
    gate = _sigmoid(_dot(x2.astype(BF16), wpg_ref[...]))
    pe = _dot(p_ref[...].astype(BF16), wpp_ref[...])
    y_ref[...] = x2 + _rms(gate * pe, gains_ref[3:4, :])


def _post(x, o_dn, o_at, p, wo, wgu, wd, wpg, wpp, gains, *, tm):
    B, S, D = x.shape
    assert S % tm == 0
    tok = lambda width: pl.BlockSpec((None, tm, width), lambda b, i: (b, i, 0))
    return pl.pallas_call(
        _post_kernel,
        grid=(B, S // tm),
        in_specs=[tok(D), tok(DN_W), pl.BlockSpec((None, AT_PAIRS, tm, LANES), lambda b, i: (b, 0, i, 0)),
                  tok(p.shape[-1]),
                  _const_spec(wo.shape), _const_spec(wgu.shape), _const_spec(wd.shape),
                  _const_spec(wpg.shape), _const_spec(wpp.shape), _const_spec(gains.shape)],
        out_specs=tok(D),
        out_shape=jax.ShapeDtypeStruct((B, S, D), F32),
        compiler_params=pltpu.CompilerParams(dimension_semantics=("arbitrary", "arbitrary"),
                                             vmem_limit_bytes=VMEM_LIMIT),
        name="post",
    )(x, o_dn, o_at, p, wo, wgu, wd, wpg, wpp, gains)


def _layer_weights(w_in, conv_w, a_log, dt_bias, g_dn_out, w_out, g_post_mix, g_pre_ffn, w_gate_up, w_down,
                   g_post_ffn, w_ple_proj, w_ple_gate, g_ple):
    c0 = CONV_CH
    c1 = c0 + DN_W
    c2 = c1 + 2 * DN_HEADS
    w_ba = jnp.pad(w_in[:, c1:c2], ((0, 0), (0, LANES - 2 * DN_HEADS)))
    w = jnp.concatenate([w_in[:, :c1], w_in[:, c2:], w_ba], axis=1).astype(BF16)
    assert w.shape[1] == IN_COLS_PAD
    ap = jnp.zeros((8, LANES), F32)
    ap = ap.at[0, DN_HEADS:2 * DN_HEADS].set(a_log.astype(F32))
    ap = ap.at[1, DN_HEADS:2 * DN_HEADS].set(dt_bias.astype(F32))
    gains = jnp.zeros((8, w_out.shape[1]), F32)
    for r, g in enumerate((g_post_mix, g_pre_ffn, g_post_ffn, g_ple)):
        gains = gains.at[r].set(g.astype(F32))
    return dict(w=w, cw=conv_w.astype(F32), ap=ap, gn=g_dn_out.astype(F32)[None], wo=w_out.astype(BF16),
                wgu=w_gate_up.astype(BF16), wd=w_down.astype(BF16), wpg=w_ple_gate.astype(BF16),
                wpp=w_ple_proj.astype(BF16), gains=gains)


def _layer(xp, xs, pp, ps, cache_conv, state, win_k, win_v, g_pre, lw, tables, stab, self_tab):
    B, S, D = xp.shape
    Bs, T, _ = xs.shape
    assert T == 1
    post_w = (lw["wo"], lw["wgu"], lw["wd"], lw["wpg"], lw["wpp"], lw["gains"])
    to_minor = lambda t: t.transpose(0, 2, 3, 1)
    from_minor = lambda t: t.transpose(0, 3, 1, 2)

    q, k, v, z, ba, qa, ka, va, wk, wv, nc = _inproj_prompt(xp, g_pre, lw["w"], lw["cw"], lw["ap"], tm=512)
    o_dn, s_fin = _delta_prompt(q, k, v, z, ba, lw["gn"], tb=256)
    o_at = _attn_prompt(qa, ka, va, tables)
    yp = _post(xp, o_dn, o_at, pp, *post_w, tm=256)

    xs2 = xs.reshape(Bs, D)
    qt, kt, v_s, z_s, ba_s, qa_s, ka_s, va_s, nc_s = _inproj_sample(
        xs2, g_pre, lw["w"], lw["cw"], lw["ap"], cache_conv.transpose(1, 0, 2))
    o_dn_s, s_new = _delta_sample(qt, kt, v_s, z_s, ba_s, lw["gn"], state)
    o_at_s, new_wk, new_wv = _attn_sample(qa_s, ka_s, va_s, to_minor(win_k), to_minor(win_v), stab, self_tab)
    o_at_s = o_at_s.reshape(1, Bs, AT_PAIRS, LANES).transpose(0, 2, 1, 3).astype(BF16)
    ys = _post(xs2[None], o_dn_s.reshape(1, Bs, DN_W).astype(BF16), o_at_s, ps.reshape(1, Bs, -1), *post_w, tm=Bs)

    n_keep = wk.shape[2]
    return (yp, ys.reshape(Bs, 1, D),
            nc[:, 8 - (CONV_W - 1):], s_fin,
            from_minor(wk.reshape(B, AT_HEADS, AT_HD, n_keep)), from_minor(wv.reshape(B, AT_HEADS, AT_HD, n_keep)),
            nc_s.transpose(1, 0, 2), s_new, from_minor(new_wk), from_minor(new_wv))


def kernel(x_prompt, x_sample, cache_conv, state_delta, cache_win_k, cache_win_v, p_prompt, p_sample, rel_bias, g_pre_mix, w_in, conv_w, a_log, dt_bias, g_dn_out, w_out, g_post_mix, g_pre_ffn, w_gate_up, w_down, g_post_ffn, w_ple_proj, w_ple_gate, g_ple):
    depth = w_in.shape[0]
    tables = _band_tables(rel_bias)
    stab, self_tab = _sample_tables(rel_bias)
    yp, ys = x_prompt, x_sample
    outs = [[] for _ in range(8)]
    for i in range(depth):
        lw = _layer_weights(w_in[i], conv_w[i], a_log[i], dt_bias[i], g_dn_out[i], w_out[i], g_post_mix[i],
                            g_pre_ffn[i], w_gate_up[i], w_down[i], g_post_ffn[i], w_ple_proj[i], w_ple_gate[i],
                            g_ple[i])
        g_pre = g_pre_mix[i].astype(F32)[None]
        yp, ys, *rest = _layer(yp, ys, p_prompt[i], p_sample[i], cache_conv[i], state_delta[i], cache_win_k[i],
                               cache_win_v[i], g_pre, lw, tables, stab, self_tab)
        for lst, val in zip(outs, rest):
            lst.append(val)
    return (yp, ys) + tuple(jnp.stack(o) for o in outs)
```

```python
import functools
import math

import numpy as np
import jax
import jax.numpy as jnp
from jax import lax
from jax.experimental import pallas as pl
from jax.experimental.pallas import tpu as pltpu

DN_HEADS = 4
DN_D = 128
CONV_W = 4
AT_HEADS = 8
AT_HD = 64
AT_PAIRS = AT_HEADS // 2
DIL_PAIRS = ((128, 1), (512, 4), (2048, 16))
WIN_MAX = 2048
BAND = 128
NUM_BUCKETS = 32
MAX_DIST = 2048
RMS_EPS = 1e-6
NEG = -1e30

DN_W = DN_HEADS * DN_D
CONV_CH = 3 * DN_W
AT_W = AT_HEADS * AT_HD
LANES = 128
SUPER = 16 * BAND

V7X_VMEM_BYTES = 64 * 1024 * 1024
VMEM_LIMIT = V7X_VMEM_BYTES - 8 * 1024 * 1024

INPROJ_TM = 512
INPROJ_SUB = 256
DELTA_TB = 256
POST_TM = 512
POST_SUB = 256
FFN_CHUNK = 256

F32 = jnp.float32
BF16 = jnp.bfloat16


def _t5_bucket(n):
    max_exact = NUM_BUCKETS // 2
    n = np.asarray(n)
    large = max_exact + (np.log(np.maximum(n, 1) / max_exact) / math.log(MAX_DIST / max_exact)
                         * (NUM_BUCKETS - max_exact)).astype(np.int32)
    large = np.minimum(large, NUM_BUCKETS - 1)
    return np.where(n < max_exact, n, large).astype(np.int32)


def _sigmoid(x):
    return 1.0 / (1.0 + jnp.exp(-x))


def _silu(x):
    return x * _sigmoid(x)


def _softplus(x):
    return jnp.maximum(x, 0.0) + jnp.log1p(jnp.exp(-jnp.abs(x)))


def _rms(x, g):
    return x * lax.rsqrt(jnp.mean(x * x, axis=-1, keepdims=True) + RMS_EPS) * g


def _dot(a, b):
    return jnp.dot(a, b, preferred_element_type=F32)


def _dot_nt(a, b):
    return lax.dot_general(a, b, (((1,), (1,)), ((), ())), preferred_element_type=F32)


def _const_spec(shape):
    nd = len(shape)
    return pl.BlockSpec(shape, lambda *_: (0,) * nd, pipeline_mode=pl.Buffered(1))


C_CONV = 0
C_Z = CONV_CH
C_Q = C_Z + DN_W
C_K = C_Q + AT_W
C_V = C_K + AT_W
C_BA = C_V + AT_W
IN_COLS_PAD = C_BA + LANES


def _dn_activations(c, ba_raw, ap_ref):
    c = _silu(c)
    qs, ks = [], []
    for h in range(DN_HEADS):
        qh = c[:, h * DN_D:(h + 1) * DN_D]
        kh = c[:, DN_W + h * DN_D:DN_W + (h + 1) * DN_D]
        qs.append(qh * lax.rsqrt(jnp.sum(qh * qh, axis=-1, keepdims=True) + 1e-6) * (DN_D ** -0.5))
        ks.append(kh * lax.rsqrt(jnp.sum(kh * kh, axis=-1, keepdims=True) + 1e-6))
    q = jnp.concatenate(qs, axis=1)
    k = jnp.concatenate(ks, axis=1)
    v = c[:, 2 * DN_W:]
    lane = lax.broadcasted_iota(jnp.int32, ba_raw.shape, 1)
    beta = _sigmoid(ba_raw)
    gdec = -jnp.exp(ap_ref[0:1, :]) * _softplus(ba_raw + ap_ref[1:2, :])
    ba = jnp.where(lane < DN_HEADS, beta, gdec)
    return q, k, v, ba


def _inproj_prompt_kernel(x_ref, g_ref, w_ref, cw_ref, ap_ref,
                          q_ref, k_ref, v_ref, z_ref, ba_ref, qa_ref, ka_ref, va_ref,
                          wk_ref, wv_ref, nc_ref, buf_ref, *, tm):
    i = pl.program_id(1)

    @pl.when(i == 0)
    def _():
        buf_ref[0:8, :] = jnp.zeros((8, CONV_CH), F32)

    sub = min(tm, INPROJ_SUB)
    subs = [slice(lo, lo + sub) for lo in range(0, tm, sub)]
    hs = [_rms(x_ref[r, :], g_ref[...]).astype(BF16) for r in subs]
    us = [_dot(h, w_ref[:, C_CONV:C_Z]) for h in hs]
    for r, u in zip(subs, us):
        buf_ref[8 + r.start:8 + r.stop, :] = u
    bas = [_dot(h, w_ref[:, C_BA:C_BA + LANES]) for h in hs]
    ctx = CONV_W - 1
    for r, h, u, ba_raw in zip(subs, hs, us, bas):
        c = u * cw_ref[ctx:CONV_W, :]
        for t in range(ctx):
            c = c + buf_ref[8 - ctx + t + r.start:8 - ctx + t + r.stop, :] * cw_ref[t:t + 1, :]
        q, k, v, ba = _dn_activations(c, ba_raw, ap_ref)
        q_ref[r, :] = q.astype(BF16)
        k_ref[r, :] = k.astype(BF16)
        v_ref[r, :] = v.astype(BF16)
        ba_ref[r, :] = ba
        z_ref[r, :] = _dot(h, w_ref[:, C_Z:C_Q]).astype(BF16)

        qa = _dot(h, w_ref[:, C_Q:C_K]) * (AT_HD ** -0.5)
        ka = _dot(h, w_ref[:, C_K:C_V])
        va = _dot(h, w_ref[:, C_V:C_BA])

        wk_ref[:, r] = ka.T
        wv_ref[:, r] = va.T

        for p in range(AT_PAIRS):
            sl = slice(p * LANES, (p + 1) * LANES)
            qa_ref[p, r, :] = qa[:, sl].astype(BF16)
            ka_ref[p, r, :] = ka[:, sl].astype(BF16)
            va_ref[p, r, :] = va[:, sl].astype(BF16)
    last = us[-1][sub - 8:sub, :]
    buf_ref[0:8, :] = last
    nc_ref[...] = last


def _inproj_prompt(x, g, w, cw, ap, *, tm):
    B, S, D = x.shape
    n_keep = min(WIN_MAX, S)
    assert S % tm == 0 and n_keep % tm == 0 and tm % 8 == 0
    nt = S // tm
    first_keep = (S - n_keep) // tm
    tok = lambda width: pl.BlockSpec((None, tm, width), lambda b, i: (b, i, 0))
    pair = pl.BlockSpec((None, AT_PAIRS, tm, LANES), lambda b, i: (b, 0, i, 0))
    keep = pl.BlockSpec((None, AT_W, tm), lambda b, i: (b, 0, jnp.maximum(i - first_keep, 0)))
    bf = lambda width: jax.ShapeDtypeStruct((B, S, width), BF16)
    pairs = jax.ShapeDtypeStruct((B, AT_PAIRS, S, LANES), BF16)
    return pl.pallas_call(
        functools.partial(_inproj_prompt_kernel, tm=tm),
        grid=(B, nt),
        in_specs=[tok(D), _const_spec((1, D)), _const_spec(w.shape), _const_spec(cw.shape),
                  _const_spec(ap.shape)],
        out_specs=[tok(DN_W), tok(DN_W), tok(DN_W), tok(DN_W), tok(LANES), pair, pair, pair,
                   keep, keep, pl.BlockSpec((None, 8, CONV_CH), lambda b, i: (b, 0, 0))],
        out_shape=[bf(DN_W), bf(DN_W), bf(DN_W), bf(DN_W), jax.ShapeDtypeStruct((B, S, LANES), F32),
                   pairs, pairs, pairs,
                   jax.ShapeDtypeStruct((B, AT_W, n_keep), F32), jax.ShapeDtypeStruct((B, AT_W, n_keep), F32),
                   jax.ShapeDtypeStruct((B, 8, CONV_CH), F32)],
        scratch_shapes=[pltpu.VMEM((tm + 8, CONV_CH), F32)],
        compiler_params=pltpu.CompilerParams(dimension_semantics=("arbitrary", "arbitrary"),
                                             vmem_limit_bytes=VMEM_LIMIT),
        name="inproj_prompt",
    )(x, g, w, cw, ap)


def _inproj_sample_kernel(x_ref, g_ref, w_ref, cw_ref, ap_ref, cc_ref,
                          qt_ref, kt_ref, v_ref, z_ref, ba_ref, qa_ref, ka_ref, va_ref, nc_ref):
    x = x_ref[...]
    h = _rms(x, g_ref[...]).astype(BF16)
    u = _dot(h, w_ref[:, C_CONV:C_Z])
    c = u * cw_ref[CONV_W - 1:CONV_W, :]
    for t in range(CONV_W - 1):
        c = c + cc_ref[t] * cw_ref[t:t + 1, :]
    for t in range(1, CONV_W - 1):
        nc_ref[t - 1] = cc_ref[t]
    nc_ref[CONV_W - 2] = u

    ba_raw = _dot(h, w_ref[:, C_BA:C_BA + LANES])
    q, k, v, ba = _dn_activations(c, ba_raw, ap_ref)
    qt_ref[...] = q.T
    kt_ref[...] = k.T
    v_ref[...] = v
    ba_ref[...] = ba
    z_ref[...] = _dot(h, w_ref[:, C_Z:C_Q])
    qa_ref[...] = _dot(h, w_ref[:, C_Q:C_K]) * (AT_HD ** -0.5)
    ka_ref[...] = _dot(h, w_ref[:, C_K:C_V])
    va_ref[...] = _dot(h, w_ref[:, C_V:C_BA])


def _inproj_sample(x, g, w, cw, ap, conv_ctx):
    B, D = x.shape
    f = lambda *s: jax.ShapeDtypeStruct(s, F32)
    return pl.pallas_call(
        _inproj_sample_kernel,
        out_shape=[f(DN_W, B), f(DN_W, B), f(B, DN_W), f(B, DN_W), f(B, LANES),
                   f(B, AT_W), f(B, AT_W), f(B, AT_W), f(CONV_W - 1, B, CONV_CH)],
        compiler_params=pltpu.CompilerParams(vmem_limit_bytes=VMEM_LIMIT),
        name="inproj_sample",
    )(x, g, w, cw, ap, conv_ctx)


DN_CHUNK = 64


def _delta_prompt_kernel(q_ref, k_ref, v_ref, z_ref, ba_ref, gn_ref, o_ref, s_out_ref, s_ref, *, nb, tb):
    i = pl.program_id(0)
    C = DN_CHUNK
    nc = tb // C

    @pl.when(i == 0)
    def _():
        s_ref[...] = jnp.zeros(s_ref.shape, F32)

    rt = lax.broadcasted_iota(jnp.int32, (tb, tb), 0)
    ct = lax.broadcasted_iota(jnp.int32, (tb, tb), 1)
    tri = jnp.where((rt >= ct) & ((rt // C) == (ct // C)), 1.0, 0.0).astype(F32)
    ri = lax.broadcasted_iota(jnp.int32, (C, C), 0)
    ci = lax.broadcasted_iota(jnp.int32, (C, C), 1)
    eye = jnp.where(ri == ci, 1.0, 0.0).astype(F32)
    gn = gn_ref[...]

    units = [(b, h, c) for b in range(nb) for h in range(DN_HEADS) for c in range(nc)]
    pre = {}
    for b in range(nb):
        ba = ba_ref[b]
        gc_all = jnp.dot(tri, ba, preferred_element_type=F32, precision=lax.Precision.HIGHEST)
        pre[b] = (ba, gc_all, gc_all.T, jnp.exp(gc_all))
    kts = {(b, h): k_ref[b, :, h * DN_D:(h + 1) * DN_D].astype(F32).T
           for b in range(nb) for h in range(DN_HEADS)}

    st = {}
    for (b, h, c) in units:
        ba, gc_all, gc_t, eg_all = pre[b]
        rows = slice(c * C, (c + 1) * C)
        lanes = slice(h * DN_D, (h + 1) * DN_D)
        gl = DN_HEADS + h
        kf = k_ref[b, rows, lanes].astype(F32)
        qf = q_ref[b, rows, lanes].astype(F32)
        vf = v_ref[b, rows, lanes].astype(F32)
        beta = ba[rows, h:h + 1]
        gcol = gc_all[rows, gl:gl + 1]
        grow = gc_t[gl:gl + 1, rows]
        egc = eg_all[rows, gl:gl + 1]
        glast = gc_all[c * C + C - 1:c * C + C, gl:gl + 1]
        kb = kf * beta
        st[b, h, c] = dict(
            dec=jnp.exp(jnp.where(ri >= ci, gcol - grow, NEG)),
            lhs=jnp.concatenate([kb, qf], axis=0).astype(BF16),
            kbf=kf.astype(BF16),
            rhs=jnp.concatenate([vf * beta, kb * egc], axis=1).astype(BF16),
            qdec=(qf * egc).astype(BF16),
            kdt=(kts[b, h][:, rows] * jnp.exp(glast - grow)).astype(BF16),
            glast=jnp.exp(glast))
    for u in units:
        st[u]["both"] = _dot_nt(st[u]["lhs"], st[u]["kbf"])
    for u in units:
        d = st[u]
        a_mat = jnp.where(ri > ci, d["both"][:C] * d["dec"], 0.0)
        d["intra"] = (d["both"][C:] * d["dec"]).astype(BF16)
        d["nb"] = (-a_mat).astype(BF16)
        d["xm"] = eye - a_mat
    for u in units:
        st[u]["pm"] = _dot(st[u]["nb"], st[u]["nb"])
    for _ in range(4):
        for u in units:
            d = st[u]
            d["r"] = _dot(jnp.concatenate([d["xm"], d["pm"]], axis=0).astype(BF16), d["pm"].astype(BF16))
        for u in units:
            d = st[u]
            d["xm"] = d["xm"] + d["r"][:C]
            d["pm"] = d["r"][C:]
    for u in units:
        d = st[u]
        d["r"] = _dot(d["xm"].astype(BF16), d["pm"].astype(BF16))
    for u in units:
        d = st[u]
        d["uw"] = _dot((d["xm"] + d["r"]).astype(BF16), d["rhs"])

    chains = [(b, h) for b in range(nb) for h in range(DN_HEADS)]
    state = {bh: s_ref[bh[0], bh[1]] for bh in chains}
    for c in range(nc):
        rows = slice(c * C, (c + 1) * C)
        r2 = {}
        for (b, h) in chains:
            d = st[b, h, c]
            lhs = jnp.concatenate([d["uw"][:, DN_D:].astype(BF16), d["qdec"]], axis=0)
            r2[b, h] = _dot(lhs, state[b, h].astype(BF16))
        vb = {bh: (st[bh[0], bh[1], c]["uw"][:, :DN_D] - r2[bh][:C]).astype(BF16) for bh in chains}
        for (b, h) in chains:
            d = st[b, h, c]
            o = r2[b, h][C:] + _dot(d["intra"], vb[b, h])
            state[b, h] = state[b, h] * d["glast"] + _dot(d["kdt"], vb[b, h])
            lanes = slice(h * DN_D, (h + 1) * DN_D)
            zf = z_ref[b, rows, lanes].astype(F32)
            o_ref[b, rows, lanes] = (_rms(o, gn) * _silu(zf)).astype(BF16)
    for (b, h) in chains:
        s_ref[b, h] = state[b, h]

    @pl.when(i == pl.num_programs(0) - 1)
    def _():
        s_out_ref[...] = s_ref[...]


def _delta_prompt(q, k, v, z, ba, gn, *, tb):
    B, S, _ = q.shape
    assert S % tb == 0 and tb % DN_CHUNK == 0
    tok = lambda width: pl.BlockSpec((B, tb, width), lambda i: (0, i, 0))
    st = pl.BlockSpec((B, DN_HEADS, DN_D, DN_D), lambda i: (0, 0, 0, 0))
    return pl.pallas_call(
        functools.partial(_delta_prompt_kernel, nb=B, tb=tb),
        grid=(S // tb,),
        in_specs=[tok(DN_W), tok(DN_W), tok(DN_W), tok(DN_W), tok(LANES), _const_spec((1, DN_D))],
        out_specs=[tok(DN_W), st],
        out_shape=[jax.ShapeDtypeStruct((B, S, DN_W), BF16),
                   jax.ShapeDtypeStruct((B, DN_HEADS, DN_D, DN_D), F32)],
        scratch_shapes=[pltpu.VMEM((B, DN_HEADS, DN_D, DN_D), F32)],
        compiler_params=pltpu.CompilerParams(dimension_semantics=("arbitrary",),
                                             vmem_limit_bytes=VMEM_LIMIT),
        name="delta_prompt",
    )(q, k, v, z, ba, gn)


SAMPLE_SEQS_PER_STEP = 8


def _delta_sample_kernel(qt_ref, kt_ref, v_ref, z_ref, ba_ref, gn_ref, s_ref, o_ref, s_out_ref):
    step = pl.program_id(0)
    nb = qt_ref.shape[1]
    per = v_ref.shape[0]
    lane = lax.broadcasted_iota(jnp.int32, (DN_D, nb), 1)
    gn = gn_ref[...]
    units = [(j, h) for j in range(per) for h in range(DN_HEADS)]
    rows = {h: slice(h * DN_D, (h + 1) * DN_D) for h in range(DN_HEADS)}
    sel = {j: lane == step * per + j for j in range(per)}
    qcol = {(j, h): jnp.sum(jnp.where(sel[j], qt_ref[rows[h], :], 0.0), axis=1, keepdims=True)
            for (j, h) in units}
    kcol = {(j, h): jnp.sum(jnp.where(sel[j], kt_ref[rows[h], :], 0.0), axis=1, keepdims=True)
            for (j, h) in units}
    egd = jnp.exp(ba_ref[...])
    s = {(j, h): s_ref[j, h] * egd[j:j + 1, DN_HEADS + h:DN_HEADS + h + 1] for (j, h) in units}
    kv = {u: jnp.sum(kcol[u] * s[u], axis=0, keepdims=True) for u in units}
    for (j, h) in units:
        delta = (v_ref[j:j + 1, rows[h]] - kv[j, h]) * ba_ref[j:j + 1, h:h + 1]
        s[j, h] = s[j, h] + kcol[j, h] * delta
        s_out_ref[j, h] = s[j, h]
    o = {u: jnp.sum(qcol[u] * s[u], axis=0, keepdims=True) for u in units}
    for (j, h) in units:
        o_ref[j:j + 1, rows[h]] = _rms(o[j, h], gn) * _silu(z_ref[j:j + 1, rows[h]])


def _delta_sample(qt, kt, v, z, ba, gn, state):
    B = v.shape[0]
    per = SAMPLE_SEQS_PER_STEP
    assert B % per == 0
    row = lambda width: pl.BlockSpec((per, width), lambda b: (b, 0))
    st = pl.BlockSpec((per, DN_HEADS, DN_D, DN_D), lambda b: (b, 0, 0, 0))
    return pl.pallas_call(
        _delta_sample_kernel,
        grid=(B // per,),
        in_specs=[_const_spec(qt.shape), _const_spec(kt.shape), row(DN_W), row(DN_W), row(LANES),
                  _const_spec((1, DN_D)), st],
        out_specs=[row(DN_W), st],
        out_shape=[jax.ShapeDtypeStruct((B, DN_W), F32), jax.ShapeDtypeStruct(state.shape, F32)],
        compiler_params=pltpu.CompilerParams(dimension_semantics=("arbitrary",)),
        name="delta_sample",
    )(qt, kt, v, z, ba, gn, state)


def _stack_position(d):
    a = np.arange(BAND)
    rows = 8 * d
    return (16 // d) * (a % rows) + a // rows


def _bias_lookup(rel_bias, bucket):
    onehot = jnp.asarray(bucket[..., None] == np.arange(NUM_BUCKETS), dtype=BF16).astype(F32)
    return jnp.einsum("...n,nh->h...", onehot, rel_bias.astype(F32), precision=lax.Precision.HIGHEST)


def _band_tables(rel_bias):
    tables = []
    for w, d in DIL_PAIRS:
        assert w // d == BAND
        pos = _stack_position(d)
        dist = pos[:, None] - np.concatenate([pos - BAND, pos])[None, :]
        valid = (dist >= 0) & (dist <= BAND)
        bucket = _t5_bucket(np.clip(dist, 0, BAND) * d)
        bias = jnp.where(valid[None], _bias_lookup(rel_bias, bucket), NEG)
        tables.append(bias.reshape(AT_PAIRS, 2 * BAND, 2 * BAND))
    return jnp.stack(tables)


def _attn_unit(q, kk, vv, table, pen):
    lane = lax.broadcasted_iota(jnp.int32, (1, LANES), 1)
    first = lane < AT_HD
    zero = jnp.zeros_like(q)
    q2 = jnp.concatenate([jnp.where(first, q, zero), jnp.where(first, zero, q)], axis=0)
    s = _dot_nt(q2, kk) + table
    if pen is not None:
        s = s + pen
    m = jnp.max(s, axis=1, keepdims=True)
    p = jnp.exp(s - m)
    l = jnp.sum(p, axis=1, keepdims=True)
    pv = _dot(p.astype(BF16), vv)
    acc = jnp.where(first, pv[:BAND], pv[BAND:])
    mt = jnp.where(first, m[:BAND], m[BAND:])
    lt = jnp.where(first, l[:BAND], l[BAND:])
    return acc, mt, lt


def _attn_prompt_kernel(q_ref, k_ref, v_ref, t_ref, o_ref,
                        k16, v16, q4, k4, v4, q1, k1, v1, acc_ref, m_ref, l_ref):
    n = pl.program_id(2)

    @pl.when(n == 0)
    def _():
        zb = jnp.zeros((BAND, LANES), BF16)
        for rho in range(16):
            k16[rho, 0:BAND, :] = zb
            v16[rho, 0:BAND, :] = zb
        for rho in range(4):
            k4[rho, 0:BAND, :] = zb
            v4[rho, 0:BAND, :] = zb
        k1[0:BAND, :] = zb
        v1[0:BAND, :] = zb

    @pl.when(n > 0)
    def _():
        for rho in range(16):
            k16[rho, 0:BAND, :] = k16[rho, BAND:2 * BAND, :]
            v16[rho, 0:BAND, :] = v16[rho, BAND:2 * BAND, :]
        for rho in range(4):
            k4[rho, 0:BAND, :] = k4[rho, 4 * BAND:5 * BAND, :]
            v4[rho, 0:BAND, :] = v4[rho, 4 * BAND:5 * BAND, :]
        k1[0:BAND, :] = k1[SUPER:SUPER + BAND, :]
        v1[0:BAND, :] = v1[SUPER:SUPER + BAND, :]

    for rho in range(16):
        cols = slice(rho * LANES, (rho + 1) * LANES)
        k16[rho, BAND:2 * BAND, :] = k_ref[:, cols]
        v16[rho, BAND:2 * BAND, :] = v_ref[:, cols]
    for rho in range(4):
        for blk in range(4):
            for j in range(4):
                src = (slice(32 * blk, 32 * blk + 32), slice((4 * j + rho) * LANES, (4 * j + rho + 1) * LANES))
                dst = slice(BAND * blk + 32 * j, BAND * blk + 32 * j + 32)
                q4[rho, dst, :] = q_ref[src]
                k4[rho, BAND + dst.start:BAND + dst.stop, :] = k_ref[src]
                v4[rho, BAND + dst.start:BAND + dst.stop, :] = v_ref[src]
    for src_ref, dst_ref, off in ((q_ref, q1, 0), (k_ref, k1, BAND), (v_ref, v1, BAND)):
        for gam in range(8):
            lo, hi = [], []
            for r in range(16):
                piece = src_ref[16 * gam:16 * gam + 16, r * LANES:(r + 1) * LANES].astype(F32)
                lo.append(piece[0:8])
                hi.append(piece[8:16])
            dst_ref[off + BAND * (2 * gam):off + BAND * (2 * gam + 1), :] = jnp.concatenate(lo, axis=0).astype(BF16)
            dst_ref[off + BAND * (2 * gam + 1):off + BAND * (2 * gam + 2), :] = jnp.concatenate(hi, axis=0).astype(BF16)

    lane2 = lax.broadcasted_iota(jnp.int32, (1, 2 * BAND), 1)
    pen = jnp.where((lane2 < BAND) & (n == 0), NEG, 0.0).astype(F32)

    t16 = t_ref[2]
    for rho in range(16):
        cols = slice(rho * LANES, (rho + 1) * LANES)
        acc, mt, lt = _attn_unit(q_ref[:, cols], k16[rho], v16[rho], t16, pen)
        acc_ref[:, cols] = acc
        m_ref[:, cols] = mt
        l_ref[:, cols] = lt

    def merge(rows, cols, acc, mt, lt, final):
        m_old = m_ref[rows, cols]
        m_new = jnp.maximum(m_old, mt)
        a = jnp.exp(m_old - m_new)
        b = jnp.exp(mt - m_new)
        acc_new = acc_ref[rows, cols] * a + acc * b
        l_new = l_ref[rows, cols] * a + lt * b
        if final:
            acc_ref[rows, cols] = acc_new / l_new
        else:
            acc_ref[rows, cols] = acc_new
            m_ref[rows, cols] = m_new
            l_ref[rows, cols] = l_new

    t4 = t_ref[1]
    for rho in range(4):
        for blk in range(4):
            acc, mt, lt = _attn_unit(q4[rho, BAND * blk:BAND * (blk + 1), :],
                                     k4[rho, BAND * blk:BAND * (blk + 2), :],
                                     v4[rho, BAND * blk:BAND * (blk + 2), :],
                                     t4, pen if blk == 0 else None)
            for j in range(4):
                sub = slice(32 * j, 32 * j + 32)
                merge(slice(32 * blk, 32 * blk + 32), slice((4 * j + rho) * LANES, (4 * j + rho + 1) * LANES),
                      acc[sub], mt[sub], lt[sub], False)

    t1 = t_ref[0]
    for blk in range(16):
        acc, mt, lt = _attn_unit(q1[BAND * blk:BAND * (blk + 1), :],
                                 k1[BAND * blk:BAND * (blk + 2), :],
                                 v1[BAND * blk:BAND * (blk + 2), :],
                                 t1, pen if blk == 0 else None)
        for r in range(16):
            sub = slice(8 * r, 8 * r + 8)
            merge(slice(8 * blk, 8 * blk + 8), slice(r * LANES, (r + 1) * LANES),
                  acc[sub], mt[sub], lt[sub], True)

    o_ref[...] = acc_ref[...].astype(BF16)


def _attn_prompt(q, k, v, tables):
    B, P, S, _ = q.shape
    assert S % SUPER == 0
    nt = S // SUPER
    view = lambda t: t.reshape(B, P, S // 16, 16 * LANES)
    tile = pl.BlockSpec((None, None, BAND, 16 * LANES), lambda b, p, n: (b, p, n, 0))
    tab = pl.BlockSpec((3, None, 2 * BAND, 2 * BAND), lambda b, p, n: (0, p, 0, 0))
    vm = lambda *s: pltpu.VMEM(s, BF16)
    out = pl.pallas_call(
        _attn_prompt_kernel,
        grid=(B, P, nt),
        in_specs=[tile, tile, tile, tab],
        out_specs=tile,
        out_shape=jax.ShapeDtypeStruct((B, P, S // 16, 16 * LANES), BF16),
        scratch_shapes=[vm(16, 2 * BAND, LANES), vm(16, 2 * BAND, LANES),
                        vm(4, 4 * BAND, LANES), vm(4, 5 * BAND, LANES), vm(4, 5 * BAND, LANES),
                        vm(SUPER, LANES), vm(SUPER + BAND, LANES), vm(SUPER + BAND, LANES),
                        pltpu.VMEM((BAND, 16 * LANES), F32), pltpu.VMEM((BAND, 16 * LANES), F32),
                        pltpu.VMEM((BAND, 16 * LANES), F32)],
        compiler_params=pltpu.CompilerParams(dimension_semantics=("arbitrary", "arbitrary", "arbitrary"),
                                             vmem_limit_bytes=VMEM_LIMIT),
        name="attn_prompt",
    )(view(q), view(k), view(v), tables)
    return out.reshape(B, P, S, LANES)


def _sample_tables(rel_bias):
    W = WIN_MAX
    j = W - np.arange(W)
    count = sum(((j % d == 0) & (j <= w)).astype(np.int64) for w, d in DIL_PAIRS)
    bias = _bias_lookup(rel_bias, _t5_bucket(j))
    tab = jnp.where(count[None] > 0, bias + np.log(np.maximum(count, 1))[None].astype(np.float32), NEG)
    self_tab = rel_bias[0].astype(F32) + math.log(len(DIL_PAIRS))
    return tab, jnp.broadcast_to(self_tab[:, None], (AT_HEADS, LANES))


SAMPLE_HEADS_PER_STEP = 4


def _attn_sample_kernel(q_ref, kn_ref, vn_ref, k_ref, v_ref, t_ref, t0_ref, o_ref, ok_ref, ov_ref):
    hb, E, W = k_ref.shape
    lane = lax.broadcasted_iota(jnp.int32, (1, W), 1)
    last = lane == W - 1
    heads = range(hb)
    qc = [q_ref[:, i:i + 1] for i in heads]
    knc = [kn_ref[:, i:i + 1] for i in heads]
    vnc = [vn_ref[:, i:i + 1] for i in heads]
    s = [jnp.sum(k_ref[i] * qc[i], axis=0, keepdims=True) + t_ref[i:i + 1, :] for i in heads]
    s0 = [jnp.sum(qc[i] * knc[i], axis=0, keepdims=True) + t0_ref[i:i + 1, 0:1] for i in heads]
    for i in heads:
        ok_ref[i] = jnp.where(last, knc[i], pltpu.roll(k_ref[i], W - 1, axis=1))
    m = [jnp.maximum(jnp.max(s[i], axis=1, keepdims=True), s0[i]) for i in heads]
    p = [jnp.exp(s[i] - m[i]) for i in heads]
    p0 = [jnp.exp(s0[i] - m[i]) for i in heads]
    l = [jnp.sum(p[i], axis=1, keepdims=True) + p0[i] for i in heads]
    acc = [jnp.sum(v_ref[i] * p[i], axis=1, keepdims=True) + p0[i] * vnc[i] for i in heads]
    for i in heads:
        ov_ref[i] = jnp.where(last, vnc[i], pltpu.roll(v_ref[i], W - 1, axis=1))
    for i in heads:
        o_ref[:, i:i + 1] = acc[i] / l[i]


def _attn_sample(q, kn, vn, win_k, win_v, tab, self_tab):
    B, H, E, W = win_k.shape
    assert W == WIN_MAX
    hb = SAMPLE_HEADS_PER_STEP
    cols = lambda t: t.reshape(B, H // hb, hb, E).transpose(0, 1, 3, 2)
    col = pl.BlockSpec((None, None, E, hb), lambda b, g: (b, g, 0, 0))
    slab = pl.BlockSpec((None, hb, E, W), lambda b, g: (b, g, 0, 0))
    o, new_k, new_v = pl.pallas_call(
        _attn_sample_kernel,
        grid=(B, H // hb),
        in_specs=[col, col, col, slab, slab,
                  pl.BlockSpec((None, hb, W), lambda b, g: (g, 0, 0)),
                  pl.BlockSpec((None, hb, LANES), lambda b, g: (g, 0, 0))],
        out_specs=[col, slab, slab],
        out_shape=[jax.ShapeDtypeStruct((B, H // hb, E, hb), F32),
                   jax.ShapeDtypeStruct(win_k.shape, F32), jax.ShapeDtypeStruct(win_v.shape, F32)],
        compiler_params=pltpu.CompilerParams(dimension_semantics=("arbitrary", "arbitrary"),
                                             vmem_limit_bytes=VMEM_LIMIT),
        name="attn_sample",
    )(cols(q), cols(kn), cols(vn), win_k, win_v, tab.reshape(H // hb, hb, W), self_tab.reshape(H // hb, hb, LANES))
    return o.transpose(0, 1, 3, 2).reshape(B, H * E), new_k, new_v


def _post_kernel(x_ref, od_ref, oa_ref, p_ref, wo_ref, wgu_ref, wd_ref, wpg_ref, wpp_ref, gains_ref, y_ref):
    ffn = wd_ref.shape[0]
    tm = x_ref.shape[0]
    sub = min(tm, POST_SUB)
    subs = [slice(lo, lo + sub) for lo in range(0, tm, sub)]
    ns = len(subs)
    mix = [_dot(jnp.concatenate([od_ref[r, :]] + [oa_ref[p, r, :] for p in range(AT_PAIRS)], axis=1), wo_ref[...])
           for r in subs]
    x1 = [x_ref[r, :] + _rms(m, gains_ref[0:1, :]) for r, m in zip(subs, mix)]
    h2 = [_rms(t, gains_ref[1:2, :]).astype(BF16) for t in x1]

    work = [(lo, s) for lo in range(0, ffn, FFN_CHUNK) for s in range(ns)]
    gate_up = {}

    def issue(lo, s):
        gate_up[lo, s] = (_dot(h2[s], wgu_ref[:, lo:lo + FFN_CHUNK]),
                          _dot(h2[s], wgu_ref[:, ffn + lo:ffn + lo + FFN_CHUNK]))

    issue(*work[0])
    down = [None] * ns
    for idx, (lo, s) in enumerate(work):
        if idx + 1 < len(work):
            issue(*work[idx + 1])
        gt, up = gate_up.pop((lo, s))
        part = _dot((_silu(gt) * up).astype(BF16), wd_ref[lo:lo + FFN_CHUNK, :])
        down[s] = part if down[s] is None else down[s] + part

    x2 = [a + _rms(d, gains_ref[2:3, :]) for a, d in zip(x1, down)]
    gate = [_dot(t.astype(BF16), wpg_ref[...]) for t in x2]
    pe = [_dot(p_ref[r, :].astype(BF16), wpp_ref[...]) for r in subs]
    for r, t, g, e in zip(subs, x2, gate, pe):
        y_ref[r, :] = t + _rms(_sigmoid(g) * e, gains_ref[3:4, :])


def _post(x, o_dn, o_at, p, wo, wgu, wd, wpg, wpp, gains, *, tm):
    B, S, D = x.shape
    assert S % tm == 0
    tok = lambda width: pl.BlockSpec((None, tm, width), lambda b, i: (b, i, 0))
    return pl.pallas_call(
        _post_kernel,
        grid=(B, S // tm),
        in_specs=[tok(D), tok(DN_W), pl.BlockSpec((None, AT_PAIRS, tm, LANES), lambda b, i: (b, 0, i, 0)),
                  tok(p.shape[-1]),
                  _const_spec(wo.shape), _const_spec(wgu.shape), _const_spec(wd.shape),
                  _const_spec(wpg.shape), _const_spec(wpp.shape), _const_spec(gains.shape)],
        out_specs=tok(D),
        out_shape=jax.ShapeDtypeStruct((B, S, D), F32),
        compiler_params=pltpu.CompilerParams(dimension_semantics=("arbitrary", "arbitrary"),
                                             vmem_limit_bytes=VMEM_LIMIT),
        name="post",
    )(x, o_dn, o_at, p, wo, wgu, wd, wpg, wpp, gains)


def _layer_weights(w_in, conv_w, a_log, dt_bias, g_dn_out, w_out, g_post_mix, g_pre_ffn, w_gate_up, w_down,
                   g_post_ffn, w_ple_proj, w_ple_gate, g_ple):
    c0 = CONV_CH
    c1 = c0 + DN_W
    c2 = c1 + 2 * DN_HEADS
    w_ba = jnp.pad(w_in[:, c1:c2], ((0, 0), (0, LANES - 2 * DN_HEADS)))
    w = jnp.concatenate([w_in[:, :c1], w_in[:, c2:], w_ba], axis=1).astype(BF16)
    assert w.shape[1] == IN_COLS_PAD
    ap = jnp.zeros((8, LANES), F32)
    ap = ap.at[0, DN_HEADS:2 * DN_HEADS].set(a_log.astype(F32))
    ap = ap.at[1, DN_HEADS:2 * DN_HEADS].set(dt_bias.astype(F32))
    gains = jnp.zeros((8, w_out.shape[1]), F32)
    for r, g in enumerate((g_post_mix, g_pre_ffn, g_post_ffn, g_ple)):
        gains = gains.at[r].set(g.astype(F32))
    return dict(w=w, cw=conv_w.astype(F32), ap=ap, gn=g_dn_out.astype(F32)[None], wo=w_out.astype(BF16),
                wgu=w_gate_up.astype(BF16), wd=w_down.astype(BF16), wpg=w_ple_gate.astype(BF16),
                wpp=w_ple_proj.astype(BF16), gains=gains)


def _layer(xp, xs, pp, ps, cache_conv, state, win_k, win_v, g_pre, lw, tables, stab, self_tab):
    B, S, D = xp.shape
    Bs, T, _ = xs.shape
    assert T == 1
    post_w = (lw["wo"], lw["wgu"], lw["wd"], lw["wpg"], lw["wpp"], lw["gains"])
    to_minor = lambda t: t.transpose(0, 2, 3, 1)
    from_minor = lambda t: t.transpose(0, 3, 1, 2)

    q, k, v, z, ba, qa, ka, va, wk, wv, nc = _inproj_prompt(xp, g_pre, lw["w"], lw["cw"], lw["ap"], tm=INPROJ_TM)
    o_dn, s_fin = _delta_prompt(q, k, v, z, ba, lw["gn"], tb=DELTA_TB)
    o_at = _attn_prompt(qa, ka, va, tables)
    yp = _post(xp, o_dn, o_at, pp, *post_w, tm=POST_TM)

    xs2 = xs.reshape(Bs, D)
    qt, kt, v_s, z_s, ba_s, qa_s, ka_s, va_s, nc_s = _inproj_sample(
        xs2, g_pre, lw["w"], lw["cw"], lw["ap"], cache_conv.transpose(1, 0, 2))
    o_dn_s, s_new = _delta_sample(qt, kt, v_s, z_s, ba_s, lw["gn"], state)
    o_at_s, new_wk, new_wv = _attn_sample(qa_s, ka_s, va_s, to_minor(win_k), to_minor(win_v), stab, self_tab)
    o_at_s = o_at_s.reshape(1, Bs, AT_PAIRS, LANES).transpose(0, 2, 1, 3).astype(BF16)
    ys = _post(xs2[None], o_dn_s.reshape(1, Bs, DN_W).astype(BF16), o_at_s, ps.reshape(1, Bs, -1), *post_w, tm=Bs)

    n_keep = wk.shape[2]
    return (yp, ys.reshape(Bs, 1, D),
            nc[:, 8 - (CONV_W - 1):], s_fin,
            from_minor(wk.reshape(B, AT_HEADS, AT_HD, n_keep)), from_minor(wv.reshape(B, AT_HEADS, AT_HD, n_keep)),
            nc_s.transpose(1, 0, 2), s_new, from_minor(new_wk), from_minor(new_wv))


def kernel(x_prompt, x_sample, cache_conv, state_delta, cache_win_k, cache_win_v, p_prompt, p_sample, rel_bias, g_pre_mix, w_in, conv_w, a_log, dt_bias, g_dn_out, w_out, g_post_mix, g_pre_ffn, w_gate_up, w_down, g_post_ffn, w_ple_proj, w_ple_gate, g_ple):
    depth = w_in.shape[0]
    tables = _band_tables(rel_bias)
    stab, self_tab = _sample_tables(rel_bias)
    yp, ys = x_prompt, x_sample
    outs = [[] for _ in range(8)]
    for i in range(depth):
        lw = _layer_weights(w_in[i], conv_w[i], a_log[i], dt_bias[i], g_dn_out[i], w_out[i], g_post_mix[i],
                            g_pre_ffn[i], w_gate_up[i], w_down[i], g_post_ffn[i], w_ple_proj[i], w_ple_gate[i],
                            g_ple[i])
        g_pre = g_pre_mix[i].astype(F32)[None]
        yp, ys, *rest = _layer(yp, ys, p_prompt[i], p_sample[i], cache_conv[i], state_delta[i], cache_win_k[i],
                               cache_win_v[i], g_pre, lw, tables, stab, self_tab)
        for lst, val in zip(outs, rest):
            lst.append(val)
    return (yp, ys) + tuple(jnp.stack(o) for o in outs)
```

```python
import functools
import math

import numpy as np
import jax
import jax.numpy as jnp
from jax import lax
from jax.experimental import pallas as pl
from jax.experimental.pallas import tpu as pltpu

DN_HEADS = 4
DN_D = 128
CONV_W = 4
AT_HEADS = 8
AT_HD = 64
AT_PAIRS = AT_HEADS // 2
DIL_PAIRS = ((128, 1), (512, 4), (2048, 16))
WIN_MAX = 2048
BAND = 128
NUM_BUCKETS = 32
MAX_DIST = 2048
RMS_EPS = 1e-6
NEG = -1e30

DN_W = DN_HEADS * DN_D
CONV_CH = 3 * DN_W
AT_W = AT_HEADS * AT_HD
LANES = 128
SUPER = 16 * BAND

V7X_VMEM_BYTES = 64 * 1024 * 1024
VMEM_LIMIT = V7X_VMEM_BYTES - 8 * 1024 * 1024

INPROJ_SUB = 256
DELTA_TB = 256
POST_TM = 512
POST_SUB = 128
FFN_CHUNK = 256

F32 = jnp.float32
BF16 = jnp.bfloat16


def _t5_bucket(n):
    max_exact = NUM_BUCKETS // 2
    n = np.asarray(n)
    large = max_exact + (np.log(np.maximum(n, 1) / max_exact) / math.log(MAX_DIST / max_exact)
                         * (NUM_BUCKETS - max_exact)).astype(np.int32)
    large = np.minimum(large, NUM_BUCKETS - 1)
    return np.where(n < max_exact, n, large).astype(np.int32)


def _sigmoid(x):
    return 1.0 / (1.0 + jnp.exp(-x))


def _silu(x):
    return x * _sigmoid(x)


def _softplus(x):
    return jnp.maximum(x, 0.0) + jnp.log1p(jnp.exp(-jnp.abs(x)))


def _rms(x, g):
    return x * lax.rsqrt(jnp.mean(x * x, axis=-1, keepdims=True) + RMS_EPS) * g


def _dot(a, b):
    return jnp.dot(a, b, preferred_element_type=F32)


def _dot_nt(a, b):
    return lax.dot_general(a, b, (((1,), (1,)), ((), ())), preferred_element_type=F32)


def _const_spec(shape):
    nd = len(shape)
    return pl.BlockSpec(shape, lambda *_: (0,) * nd, pipeline_mode=pl.Buffered(1))


C_CONV = 0
C_Z = CONV_CH
C_Q = C_Z + DN_W
C_K = C_Q + AT_W
C_V = C_K + AT_W
C_BA = C_V + AT_W
IN_COLS_PAD = C_BA + LANES


def _dn_activations(c, ba_raw, ap_ref):
    c = _silu(c)
    qs, ks = [], []
    for h in range(DN_HEADS):
        qh = c[:, h * DN_D:(h + 1) * DN_D]
        kh = c[:, DN_W + h * DN_D:DN_W + (h + 1) * DN_D]
        qs.append(qh * lax.rsqrt(jnp.sum(qh * qh, axis=-1, keepdims=True) + 1e-6) * (DN_D ** -0.5))
        ks.append(kh * lax.rsqrt(jnp.sum(kh * kh, axis=-1, keepdims=True) + 1e-6))
    q = jnp.concatenate(qs, axis=1)
    k = jnp.concatenate(ks, axis=1)
    v = c[:, 2 * DN_W:]
    lane = lax.broadcasted_iota(jnp.int32, ba_raw.shape, 1)
    beta = _sigmoid(ba_raw)
    gdec = -jnp.exp(ap_ref[0:1, :]) * _softplus(ba_raw + ap_ref[1:2, :])
    ba = jnp.where(lane < DN_HEADS, beta, gdec)
    return q, k, v, ba


def _inproj_prompt_kernel(x_ref, g_ref, w_ref, cw_ref, ap_ref, sq_ref, skn_ref, swin_ref, stab_ref, stab0_ref,
                          q_ref, k_ref, v_ref, z_ref, ba_ref, qa_ref, ka_ref, va_ref,
                          wk_ref, wv_ref, nc_ref, swin_out_ref, ss_ref, ss0_ref, buf_ref, *, tm):
    i = pl.program_id(1)

    @pl.when(i == 0)
    def _():
        buf_ref[0:8, :] = jnp.zeros((8, CONV_CH), F32)

    sub = min(tm, INPROJ_SUB)
    subs = [slice(lo, lo + sub) for lo in range(0, tm, sub)]
    hs = [_rms(x_ref[r, :], g_ref[...]).astype(BF16) for r in subs]
    us = [_dot(h, w_ref[:, C_CONV:C_Z]) for h in hs]
    for r, u in zip(subs, us):
        buf_ref[8 + r.start:8 + r.stop, :] = u
    bas = [_dot(h, w_ref[:, C_BA:C_BA + LANES]) for h in hs]
    ctx = CONV_W - 1
    for r, h, u, ba_raw in zip(subs, hs, us, bas):
        c = u * cw_ref[ctx:CONV_W, :]
        for t in range(ctx):
            c = c + buf_ref[8 - ctx + t + r.start:8 - ctx + t + r.stop, :] * cw_ref[t:t + 1, :]
        q, k, v, ba = _dn_activations(c, ba_raw, ap_ref)
        q_ref[r, :] = q.astype(BF16)
        k_ref[r, :] = k.astype(BF16)
        v_ref[r, :] = v.astype(BF16)
        ba_ref[r, :] = ba
        z_ref[r, :] = _dot(h, w_ref[:, C_Z:C_Q]).astype(BF16)

        qa = _dot(h, w_ref[:, C_Q:C_K]) * (AT_HD ** -0.5)
        ka = _dot(h, w_ref[:, C_K:C_V])
        va = _dot(h, w_ref[:, C_V:C_BA])

        wk_ref[:, r] = ka.T
        wv_ref[:, r] = va.T

        for p in range(AT_PAIRS):
            sl = slice(p * LANES, (p + 1) * LANES)
            qa_ref[p, r, :] = qa[:, sl].astype(BF16)
            ka_ref[p, r, :] = ka[:, sl].astype(BF16)
            va_ref[p, r, :] = va[:, sl].astype(BF16)
    last = us[-1][sub - 8:sub, :]
    buf_ref[0:8, :] = last
    nc_ref[...] = last

    _window_keys(sq_ref, skn_ref, swin_ref, stab_ref, stab0_ref, swin_out_ref, ss_ref, ss0_ref)


def _inproj_prompt(x, g, w, cw, ap, s_q, s_kn, s_win_k, s_tab, s_tab0):
    B, S, D = x.shape
    Bs, H, E, W = s_win_k.shape
    n_keep = min(WIN_MAX, S)
    tm = B * S // Bs
    assert B * S == Bs * tm and S % tm == 0 and n_keep % tm == 0 and tm % 8 == 0
    nt = S // tm
    first_keep = (S - n_keep) // tm
    col, slab, score, score0 = _window_specs(nt)
    tok = lambda width: pl.BlockSpec((None, tm, width), lambda b, i: (b, i, 0))
    pair = pl.BlockSpec((None, AT_PAIRS, tm, LANES), lambda b, i: (b, 0, i, 0))
    keep = pl.BlockSpec((None, AT_W, tm), lambda b, i: (b, 0, jnp.maximum(i - first_keep, 0)))
    bf = lambda width: jax.ShapeDtypeStruct((B, S, width), BF16)
    pairs = jax.ShapeDtypeStruct((B, AT_PAIRS, S, LANES), BF16)
    return pl.pallas_call(
        functools.partial(_inproj_prompt_kernel, tm=tm),
        grid=(B, nt),
        in_specs=[tok(D), _const_spec((1, D)), _const_spec(w.shape), _const_spec(cw.shape),
                  _const_spec(ap.shape), col, col, slab, _const_spec(s_tab.shape), _const_spec(s_tab0.shape)],
        out_specs=[tok(DN_W), tok(DN_W), tok(DN_W), tok(DN_W), tok(LANES), pair, pair, pair,
                   keep, keep, pl.BlockSpec((None, 8, CONV_CH), lambda b, i: (b, 0, 0)),
                   slab, score, score0],
        out_shape=[bf(DN_W), bf(DN_W), bf(DN_W), bf(DN_W), jax.ShapeDtypeStruct((B, S, LANES), F32),
                   pairs, pairs, pairs,
                   jax.ShapeDtypeStruct((B, AT_W, n_keep), F32), jax.ShapeDtypeStruct((B, AT_W, n_keep), F32),
                   jax.ShapeDtypeStruct((B, 8, CONV_CH), F32),
                   jax.ShapeDtypeStruct(s_win_k.shape, F32), jax.ShapeDtypeStruct((Bs, H, W), F32),
                   jax.ShapeDtypeStruct((Bs, H, LANES), F32)],
        scratch_shapes=[pltpu.VMEM((tm + 8, CONV_CH), F32)],
        compiler_params=pltpu.CompilerParams(dimension_semantics=("arbitrary", "arbitrary"),
                                             vmem_limit_bytes=VMEM_LIMIT),
        name="inproj_prompt",
    )(x, g, w, cw, ap, s_q, s_kn, s_win_k, s_tab, s_tab0)


def _inproj_sample_kernel(x_ref, g_ref, w_ref, cw_ref, ap_ref, cc_ref,
                          qt_ref, kt_ref, v_ref, z_ref, ba_ref, qa_ref, ka_ref, va_ref, nc_ref):
    x = x_ref[...]
    h = _rms(x, g_ref[...]).astype(BF16)
    u = _dot(h, w_ref[:, C_CONV:C_Z])
    c = u * cw_ref[CONV_W - 1:CONV_W, :]
    for t in range(CONV_W - 1):
        c = c + cc_ref[t] * cw_ref[t:t + 1, :]
    for t in range(1, CONV_W - 1):
        nc_ref[t - 1] = cc_ref[t]
    nc_ref[CONV_W - 2] = u

    ba_raw = _dot(h, w_ref[:, C_BA:C_BA + LANES])
    q, k, v, ba = _dn_activations(c, ba_raw, ap_ref)
    qt_ref[...] = q.T
    kt_ref[...] = k.T
    v_ref[...] = v
    ba_ref[...] = ba
    z_ref[...] = _dot(h, w_ref[:, C_Z:C_Q])
    qa_ref[...] = _dot(h, w_ref[:, C_Q:C_K]) * (AT_HD ** -0.5)
    ka_ref[...] = _dot(h, w_ref[:, C_K:C_V])
    va_ref[...] = _dot(h, w_ref[:, C_V:C_BA])


def _inproj_sample(x, g, w, cw, ap, conv_ctx):
    B, D = x.shape
    f = lambda *s: jax.ShapeDtypeStruct(s, F32)
    return pl.pallas_call(
        _inproj_sample_kernel,
        out_shape=[f(DN_W, B), f(DN_W, B), f(B, DN_W), f(B, DN_W), f(B, LANES),
                   f(B, AT_W), f(B, AT_W), f(B, AT_W), f(CONV_W - 1, B, CONV_CH)],
        compiler_params=pltpu.CompilerParams(vmem_limit_bytes=VMEM_LIMIT),
        name="inproj_sample",
    )(x, g, w, cw, ap, conv_ctx)


DN_CHUNK = 64


def _delta_prompt_kernel(q_ref, k_ref, v_ref, z_ref, ba_ref, gn_ref, o_ref, s_out_ref, s_ref, *, nb, tb):
    i = pl.program_id(0)
    C = DN_CHUNK
    nc = tb // C

    @pl.when(i == 0)
    def _():
        s_ref[...] = jnp.zeros(s_ref.shape, F32)

    rt = lax.broadcasted_iota(jnp.int32, (tb, tb), 0)
    ct = lax.broadcasted_iota(jnp.int32, (tb, tb), 1)
    tri = jnp.where((rt >= ct) & ((rt // C) == (ct // C)), 1.0, 0.0).astype(F32)
    ri = lax.broadcasted_iota(jnp.int32, (C, C), 0)
    ci = lax.broadcasted_iota(jnp.int32, (C, C), 1)
    eye = jnp.where(ri == ci, 1.0, 0.0).astype(F32)
    gn = gn_ref[...]

    units = [(b, h, c) for b in range(nb) for h in range(DN_HEADS) for c in range(nc)]
    pre = {}
    for b in range(nb):
        ba = ba_ref[b]
        gc_all = jnp.dot(tri, ba, preferred_element_type=F32, precision=lax.Precision.HIGHEST)
        pre[b] = (ba, gc_all, gc_all.T, jnp.exp(gc_all))
    kts = {(b, h): k_ref[b, :, h * DN_D:(h + 1) * DN_D].astype(F32).T
           for b in range(nb) for h in range(DN_HEADS)}

    st = {}
    for (b, h, c) in units:
        ba, gc_all, gc_t, eg_all = pre[b]
        rows = slice(c * C, (c + 1) * C)
        lanes = slice(h * DN_D, (h + 1) * DN_D)
        gl = DN_HEADS + h
        kf = k_ref[b, rows, lanes].astype(F32)
        qf = q_ref[b, rows, lanes].astype(F32)
        vf = v_ref[b, rows, lanes].astype(F32)
        beta = ba[rows, h:h + 1]
        gcol = gc_all[rows, gl:gl + 1]
        grow = gc_t[gl:gl + 1, rows]
        egc = eg_all[rows, gl:gl + 1]
        glast = gc_all[c * C + C - 1:c * C + C, gl:gl + 1]
        kb = kf * beta
        st[b, h, c] = dict(
            dec=jnp.exp(jnp.where(ri >= ci, gcol - grow, NEG)),
            lhs=jnp.concatenate([kb, qf], axis=0).astype(BF16),
            kbf=kf.astype(BF16),
            rhs=jnp.concatenate([vf * beta, kb * egc], axis=1).astype(BF16),
            qdec=(qf * egc).astype(BF16),
            kdt=(kts[b, h][:, rows] * jnp.exp(glast - grow)).astype(BF16),
            glast=jnp.exp(glast))
    for u in units:
        st[u]["both"] = _dot_nt(st[u]["lhs"], st[u]["kbf"])
    for u in units:
        d = st[u]
        a_mat = jnp.where(ri > ci, d["both"][:C] * d["dec"], 0.0)
        d["intra"] = (d["both"][C:] * d["dec"]).astype(BF16)
        d["nb"] = (-a_mat).astype(BF16)
        d["xm"] = eye - a_mat
    for u in units:
        st[u]["pm"] = _dot(st[u]["nb"], st[u]["nb"])
    for _ in range(4):
        for u in units:
            d = st[u]
            d["r"] = _dot(jnp.concatenate([d["xm"], d["pm"]], axis=0).astype(BF16), d["pm"].astype(BF16))
        for u in units:
            d = st[u]
            d["xm"] = d["xm"] + d["r"][:C]
            d["pm"] = d["r"][C:]
    for u in units:
        d = st[u]
        d["r"] = _dot(d["xm"].astype(BF16), d["pm"].astype(BF16))
    for u in units:
        d = st[u]
        d["uw"] = _dot((d["xm"] + d["r"]).astype(BF16), d["rhs"])

    chains = [(b, h) for b in range(nb) for h in range(DN_HEADS)]
    state = {bh: s_ref[bh[0], bh[1]] for bh in chains}
    for c in range(nc):
        rows = slice(c * C, (c + 1) * C)
        r2 = {}
        for (b, h) in chains:
            d = st[b, h, c]
            lhs = jnp.concatenate([d["uw"][:, DN_D:].astype(BF16), d["qdec"]], axis=0)
            r2[b, h] = _dot(lhs, state[b, h].astype(BF16))
        vb = {bh: (st[bh[0], bh[1], c]["uw"][:, :DN_D] - r2[bh][:C]).astype(BF16) for bh in chains}
        for (b, h) in chains:
            d = st[b, h, c]
            o = r2[b, h][C:] + _dot(d["intra"], vb[b, h])
            state[b, h] = state[b, h] * d["glast"] + _dot(d["kdt"], vb[b, h])
            lanes = slice(h * DN_D, (h + 1) * DN_D)
            zf = z_ref[b, rows, lanes].astype(F32)
            o_ref[b, rows, lanes] = (_rms(o, gn) * _silu(zf)).astype(BF16)
    for (b, h) in chains:
        s_ref[b, h] = state[b, h]

    @pl.when(i == pl.num_programs(0) - 1)
    def _():
        s_out_ref[...] = s_ref[...]


def _delta_prompt(q, k, v, z, ba, gn, *, tb):
    B, S, _ = q.shape
    assert S % tb == 0 and tb % DN_CHUNK == 0
    tok = lambda width: pl.BlockSpec((B, tb, width), lambda i: (0, i, 0))
    st = pl.BlockSpec((B, DN_HEADS, DN_D, DN_D), lambda i: (0, 0, 0, 0))
    return pl.pallas_call(
        functools.partial(_delta_prompt_kernel, nb=B, tb=tb),
        grid=(S // tb,),
        in_specs=[tok(DN_W), tok(DN_W), tok(DN_W), tok(DN_W), tok(LANES), _const_spec((1, DN_D))],
        out_specs=[tok(DN_W), st],
        out_shape=[jax.ShapeDtypeStruct((B, S, DN_W), BF16),
                   jax.ShapeDtypeStruct((B, DN_HEADS, DN_D, DN_D), F32)],
        scratch_shapes=[pltpu.VMEM((B, DN_HEADS, DN_D, DN_D), F32)],
        compiler_params=pltpu.CompilerParams(dimension_semantics=("arbitrary",),
                                             vmem_limit_bytes=VMEM_LIMIT),
        name="delta_prompt",
    )(q, k, v, z, ba, gn)


SAMPLE_SEQS_PER_STEP = 8


def _delta_sample_kernel(qt_ref, kt_ref, v_ref, z_ref, ba_ref, gn_ref, s_ref, o_ref, s_out_ref):
    step = pl.program_id(0)
    nb = qt_ref.shape[1]
    per = v_ref.shape[0]
    lane = lax.broadcasted_iota(jnp.int32, (DN_D, nb), 1)
    gn = gn_ref[...]
    units = [(j, h) for j in range(per) for h in range(DN_HEADS)]
    rows = {h: slice(h * DN_D, (h + 1) * DN_D) for h in range(DN_HEADS)}
    sel = {j: lane == step * per + j for j in range(per)}
    qcol = {(j, h): jnp.sum(jnp.where(sel[j], qt_ref[rows[h], :], 0.0), axis=1, keepdims=True)
            for (j, h) in units}
    kcol = {(j, h): jnp.sum(jnp.where(sel[j], kt_ref[rows[h], :], 0.0), axis=1, keepdims=True)
            for (j, h) in units}
    egd = jnp.exp(ba_ref[...])
    s = {(j, h): s_ref[j, h] * egd[j:j + 1, DN_HEADS + h:DN_HEADS + h + 1] for (j, h) in units}
    kv = {u: jnp.sum(kcol[u] * s[u], axis=0, keepdims=True) for u in units}
    for (j, h) in units:
        delta = (v_ref[j:j + 1, rows[h]] - kv[j, h]) * ba_ref[j:j + 1, h:h + 1]
        s[j, h] = s[j, h] + kcol[j, h] * delta
        s_out_ref[j, h] = s[j, h]
    o = {u: jnp.sum(qcol[u] * s[u], axis=0, keepdims=True) for u in units}
    for (j, h) in units:
        o_ref[j:j + 1, rows[h]] = _rms(o[j, h], gn) * _silu(z_ref[j:j + 1, rows[h]])


def _delta_sample(qt, kt, v, z, ba, gn, state):
    B = v.shape[0]
    per = SAMPLE_SEQS_PER_STEP
    assert B % per == 0
    row = lambda width: pl.BlockSpec((per, width), lambda b: (b, 0))
    st = pl.BlockSpec((per, DN_HEADS, DN_D, DN_D), lambda b: (b, 0, 0, 0))
    return pl.pallas_call(
        _delta_sample_kernel,
        grid=(B // per,),
        in_specs=[_const_spec(qt.shape), _const_spec(kt.shape), row(DN_W), row(DN_W), row(LANES),
                  _const_spec((1, DN_D)), st],
        out_specs=[row(DN_W), st],
        out_shape=[jax.ShapeDtypeStruct((B, DN_W), F32), jax.ShapeDtypeStruct(state.shape, F32)],
        compiler_params=pltpu.CompilerParams(dimension_semantics=("arbitrary",)),
        name="delta_sample",
    )(qt, kt, v, z, ba, gn, state)


def _stack_position(d):
    a = np.arange(BAND)
    rows = 8 * d
    return (16 // d) * (a % rows) + a // rows


def _bias_lookup(rel_bias, bucket):
    onehot = jnp.asarray(bucket[..., None] == np.arange(NUM_BUCKETS), dtype=BF16).astype(F32)
    return jnp.einsum("...n,nh->h...", onehot, rel_bias.astype(F32), precision=lax.Precision.HIGHEST)


def _band_tables(rel_bias):
    tables = []
    for w, d in DIL_PAIRS:
        assert w // d == BAND
        pos = _stack_position(d)
        dist = pos[:, None] - np.concatenate([pos - BAND, pos])[None, :]
        valid = (dist >= 0) & (dist <= BAND)
        bucket = _t5_bucket(np.clip(dist, 0, BAND) * d)
        bias = jnp.where(valid[None], _bias_lookup(rel_bias, bucket), NEG)
        tables.append(bias.reshape(AT_PAIRS, 2 * BAND, 2 * BAND))
    return jnp.stack(tables)


def _attn_unit(q, kk, vv, table, pen):
    lane = lax.broadcasted_iota(jnp.int32, (1, LANES), 1)
    first = lane < AT_HD
    zero = jnp.zeros_like(q)
    q2 = jnp.concatenate([jnp.where(first, q, zero), jnp.where(first, zero, q)], axis=0)
    s = _dot_nt(q2, kk) + table
    if pen is not None:
        s = s + pen
    m = jnp.max(s, axis=1, keepdims=True)
    p = jnp.exp(s - m)
    l = jnp.sum(p, axis=1, keepdims=True)
    pv = _dot(p.astype(BF16), vv)
    acc = jnp.where(first, pv[:BAND], pv[BAND:])
    mt = jnp.where(first, m[:BAND], m[BAND:])
    lt = jnp.where(first, l[:BAND], l[BAND:])
    return acc, mt, lt


def _attn_prompt_kernel(q_ref, k_ref, v_ref, t_ref, o_ref,
                        k16, v16, q4, k4, v4, q1, k1, v1, acc_ref, m_ref, l_ref):
    n = pl.program_id(2)

    @pl.when(n == 0)
    def _():
        zb = jnp.zeros((BAND, LANES), BF16)
        for rho in range(16):
            k16[rho, 0:BAND, :] = zb
            v16[rho, 0:BAND, :] = zb
        for rho in range(4):
            k4[rho, 0:BAND, :] = zb
            v4[rho, 0:BAND, :] = zb
        k1[0:BAND, :] = zb
        v1[0:BAND, :] = zb

    @pl.when(n > 0)
    def _():
        for rho in range(16):
            k16[rho, 0:BAND, :] = k16[rho, BAND:2 * BAND, :]
            v16[rho, 0:BAND, :] = v16[rho, BAND:2 * BAND, :]
        for rho in range(4):
            k4[rho, 0:BAND, :] = k4[rho, 4 * BAND:5 * BAND, :]
            v4[rho, 0:BAND, :] = v4[rho, 4 * BAND:5 * BAND, :]
        k1[0:BAND, :] = k1[SUPER:SUPER + BAND, :]
        v1[0:BAND, :] = v1[SUPER:SUPER + BAND, :]

    for rho in range(16):
        cols = slice(rho * LANES, (rho + 1) * LANES)
        k16[rho, BAND:2 * BAND, :] = k_ref[:, cols]
        v16[rho, BAND:2 * BAND, :] = v_ref[:, cols]
    for rho in range(4):
        for blk in range(4):
            for j in range(4):
                src = (slice(32 * blk, 32 * blk + 32), slice((4 * j + rho) * LANES, (4 * j + rho + 1) * LANES))
                dst = slice(BAND * blk + 32 * j, BAND * blk + 32 * j + 32)
                q4[rho, dst, :] = q_ref[src]
                k4[rho, BAND + dst.start:BAND + dst.stop, :] = k_ref[src]
                v4[rho, BAND + dst.start:BAND + dst.stop, :] = v_ref[src]
    for src_ref, dst_ref, off in ((q_ref, q1, 0), (k_ref, k1, BAND), (v_ref, v1, BAND)):
        for gam in range(8):
            lo, hi = [], []
            for r in range(16):
                piece = src_ref[16 * gam:16 * gam + 16, r * LANES:(r + 1) * LANES].astype(F32)
                lo.append(piece[0:8])
                hi.append(piece[8:16])
            dst_ref[off + BAND * (2 * gam):off + BAND * (2 * gam + 1), :] = jnp.concatenate(lo, axis=0).astype(BF16)
            dst_ref[off + BAND * (2 * gam + 1):off + BAND * (2 * gam + 2), :] = jnp.concatenate(hi, axis=0).astype(BF16)

    lane2 = lax.broadcasted_iota(jnp.int32, (1, 2 * BAND), 1)
    pen = jnp.where((lane2 < BAND) & (n == 0), NEG, 0.0).astype(F32)

    t16 = t_ref[2]
    for rho in range(16):
        cols = slice(rho * LANES, (rho + 1) * LANES)
        acc, mt, lt = _attn_unit(q_ref[:, cols], k16[rho], v16[rho], t16, pen)
        acc_ref[:, cols] = acc
        m_ref[:, cols] = mt
        l_ref[:, cols] = lt

    def merge(rows, cols, acc, mt, lt, final):
        m_old = m_ref[rows, cols]
        m_new = jnp.maximum(m_old, mt)
        a = jnp.exp(m_old - m_new)
        b = jnp.exp(mt - m_new)
        acc_new = acc_ref[rows, cols] * a + acc * b
        l_new = l_ref[rows, cols] * a + lt * b
        if final:
            acc_ref[rows, cols] = acc_new / l_new
        else:
            acc_ref[rows, cols] = acc_new
            m_ref[rows, cols] = m_new
            l_ref[rows, cols] = l_new

    t4 = t_ref[1]
    for rho in range(4):
        for blk in range(4):
            acc, mt, lt = _attn_unit(q4[rho, BAND * blk:BAND * (blk + 1), :],
                                     k4[rho, BAND * blk:BAND * (blk + 2), :],
                                     v4[rho, BAND * blk:BAND * (blk + 2), :],
                                     t4, pen if blk == 0 else None)
            for j in range(4):
                sub = slice(32 * j, 32 * j + 32)
                merge(slice(32 * blk, 32 * blk + 32), slice((4 * j + rho) * LANES, (4 * j + rho + 1) * LANES),
                      acc[sub], mt[sub], lt[sub], False)

    t1 = t_ref[0]
    for blk in range(16):
        acc, mt, lt = _attn_unit(q1[BAND * blk:BAND * (blk + 1), :],
                                 k1[BAND * blk:BAND * (blk + 2), :],
                                 v1[BAND * blk:BAND * (blk + 2), :],
                                 t1, pen if blk == 0 else None)
        for r in range(16):
            sub = slice(8 * r, 8 * r + 8)
            merge(slice(8 * blk, 8 * blk + 8), slice(r * LANES, (r + 1) * LANES),
                  acc[sub], mt[sub], lt[sub], True)

    o_ref[...] = acc_ref[...].astype(BF16)


def _attn_prompt(q, k, v, tables):
    B, P, S, _ = q.shape
    assert S % SUPER == 0
    nt = S // SUPER
    view = lambda t: t.reshape(B, P, S // 16, 16 * LANES)
    tile = pl.BlockSpec((None, None, BAND, 16 * LANES), lambda b, p, n: (b, p, n, 0))
    tab = pl.BlockSpec((3, None, 2 * BAND, 2 * BAND), lambda b, p, n: (0, p, 0, 0))
    vm = lambda *s: pltpu.VMEM(s, BF16)
    out = pl.pallas_call(
        _attn_prompt_kernel,
        grid=(B, P, nt),
        in_specs=[tile, tile, tile, tab],
        out_specs=tile,
        out_shape=jax.ShapeDtypeStruct((B, P, S // 16, 16 * LANES), BF16),
        scratch_shapes=[vm(16, 2 * BAND, LANES), vm(16, 2 * BAND, LANES),
                        vm(4, 4 * BAND, LANES), vm(4, 5 * BAND, LANES), vm(4, 5 * BAND, LANES),
                        vm(SUPER, LANES), vm(SUPER + BAND, LANES), vm(SUPER + BAND, LANES),
                        pltpu.VMEM((BAND, 16 * LANES), F32), pltpu.VMEM((BAND, 16 * LANES), F32),
                        pltpu.VMEM((BAND, 16 * LANES), F32)],
        compiler_params=pltpu.CompilerParams(dimension_semantics=("arbitrary", "arbitrary", "arbitrary"),
                                             vmem_limit_bytes=VMEM_LIMIT),
        name="attn_prompt",
    )(view(q), view(k), view(v), tables)
    return out.reshape(B, P, S, LANES)


def _sample_tables(rel_bias):
    W = WIN_MAX
    j = W - np.arange(W)
    count = sum(((j % d == 0) & (j <= w)).astype(np.int64) for w, d in DIL_PAIRS)
    bias = _bias_lookup(rel_bias, _t5_bucket(j))
    tab = jnp.where(count[None] > 0, bias + np.log(np.maximum(count, 1))[None].astype(np.float32), NEG)
    self_tab = rel_bias[0].astype(F32) + math.log(len(DIL_PAIRS))
    return tab, jnp.broadcast_to(self_tab[:, None], (AT_HEADS, LANES))


def _window_keys(q_ref, kn_ref, k_ref, t_ref, t0_ref, ok_ref, s_ref, s0_ref):
    H, E, W = k_ref.shape
    last = lax.broadcasted_iota(jnp.int32, (1, W), 1) == W - 1
    for h in range(H):
        qc = q_ref[:, h:h + 1]
        knc = kn_ref[:, h:h + 1]
        kh = k_ref[h]
        s_ref[h:h + 1, :] = jnp.sum(kh * qc, axis=0, keepdims=True) + t_ref[h:h + 1, :]
        s0_ref[h:h + 1, :] = jnp.sum(qc * knc, axis=0, keepdims=True) + t0_ref[h:h + 1, :]
        ok_ref[h] = jnp.where(last, knc, pltpu.roll(kh, W - 1, axis=1))


def _window_values(s_ref, s0_ref, vn_ref, v_ref, ov_ref, o_ref):
    H, E, W = v_ref.shape
    last = lax.broadcasted_iota(jnp.int32, (1, W), 1) == W - 1
    s = s_ref[...]
    s0 = s0_ref[:, 0:1]
    m = jnp.maximum(jnp.max(s, axis=1, keepdims=True), s0)
    p = jnp.exp(s - m)
    p0 = jnp.exp(s0 - m)
    l = jnp.sum(p, axis=1, keepdims=True) + p0
    for h in range(H):
        vh = v_ref[h]
        vnc = vn_ref[:, h:h + 1]
        acc = jnp.sum(vh * p[h:h + 1, :], axis=1, keepdims=True) + p0[h:h + 1, :] * vnc
        o_ref[:, h:h + 1] = acc / l[h:h + 1, :]
        ov_ref[h] = jnp.where(last, vnc, pltpu.roll(vh, W - 1, axis=1))


def _window_specs(nt):
    seq = lambda b, i: b * nt + i
    col = pl.BlockSpec((None, AT_HD, AT_HEADS), lambda b, i: (seq(b, i), 0, 0))
    slab = pl.BlockSpec((None, AT_HEADS, AT_HD, WIN_MAX), lambda b, i: (seq(b, i), 0, 0, 0))
    score = pl.BlockSpec((None, AT_HEADS, WIN_MAX), lambda b, i: (seq(b, i), 0, 0))
    score0 = pl.BlockSpec((None, AT_HEADS, LANES), lambda b, i: (seq(b, i), 0, 0))
    return col, slab, score, score0


def _post_kernel(x_ref, od_ref, oa_ref, p_ref, wo_ref, wgu_ref, wd_ref, wpg_ref, wpp_ref, gains_ref, *rest):
    if len(rest) == 1:
        (y_ref,) = rest
    else:
        ss_ref, ss0_ref, svn_ref, swin_ref, y_ref, swin_out_ref, so_ref = rest
        _window_values(ss_ref, ss0_ref, svn_ref, swin_ref, swin_out_ref, so_ref)
    ffn = wd_ref.shape[0]
    tm = x_ref.shape[0]
    sub = min(tm, POST_SUB)
    subs = [slice(lo, lo + sub) for lo in range(0, tm, sub)]
    ns = len(subs)
    mix = [_dot(jnp.concatenate([od_ref[r, :]] + [oa_ref[p, r, :] for p in range(AT_PAIRS)], axis=1), wo_ref[...])
           for r in subs]
    x1 = [x_ref[r, :] + _rms(m, gains_ref[0:1, :]) for r, m in zip(subs, mix)]
    h2 = [_rms(t, gains_ref[1:2, :]).astype(BF16) for t in x1]

    work = [(lo, s) for lo in range(0, ffn, FFN_CHUNK) for s in range(ns)]
    gate_up = {}

    def issue(lo, s):
        gate_up[lo, s] = (_dot(h2[s], wgu_ref[:, lo:lo + FFN_CHUNK]),
                          _dot(h2[s], wgu_ref[:, ffn + lo:ffn + lo + FFN_CHUNK]))

    issue(*work[0])
    down = [None] * ns
    for idx, (lo, s) in enumerate(work):
        if idx + 1 < len(work):
            issue(*work[idx + 1])
        gt, up = gate_up.pop((lo, s))
        part = _dot((_silu(gt) * up).astype(BF16), wd_ref[lo:lo + FFN_CHUNK, :])
        down[s] = part if down[s] is None else down[s] + part

    x2 = [a + _rms(d, gains_ref[2:3, :]) for a, d in zip(x1, down)]
    gate = [_dot(t.astype(BF16), wpg_ref[...]) for t in x2]
    pe = [_dot(p_ref[r, :].astype(BF16), wpp_ref[...]) for r in subs]
    for r, t, g, e in zip(subs, x2, gate, pe):
        y_ref[r, :] = t + _rms(_sigmoid(g) * e, gains_ref[3:4, :])


def _post(x, o_dn, o_at, p, wo, wgu, wd, wpg, wpp, gains, window=None):
    B, S, D = x.shape
    tm = min(S, POST_TM) if window is None else B * S // window[3].shape[0]
    assert S % tm == 0
    nt = S // tm
    tok = lambda width: pl.BlockSpec((None, tm, width), lambda b, i: (b, i, 0))
    in_specs = [tok(D), tok(DN_W), pl.BlockSpec((None, AT_PAIRS, tm, LANES), lambda b, i: (b, 0, i, 0)),
                tok(p.shape[-1]),
                _const_spec(wo.shape), _const_spec(wgu.shape), _const_spec(wd.shape),
                _const_spec(wpg.shape), _const_spec(wpp.shape), _const_spec(gains.shape)]
    out_specs = [tok(D)]
    out_shape = [jax.ShapeDtypeStruct((B, S, D), F32)]
    args = [x, o_dn, o_at, p, wo, wgu, wd, wpg, wpp, gains]
    if window is not None:
        Bs, H, E, W = window[3].shape
        assert B * S == Bs * tm
        col, slab, score, score0 = _window_specs(nt)
        in_specs += [score, score0, col, slab]
        out_specs += [slab, col]
        out_shape += [jax.ShapeDtypeStruct((Bs, H, E, W), F32), jax.ShapeDtypeStruct((Bs, E, H), F32)]
        args += list(window)
    out = pl.pallas_call(
        _post_kernel,
        grid=(B, nt),
        in_specs=in_specs,
        out_specs=out_specs,
        out_shape=out_shape,
        compiler_params=pltpu.CompilerParams(dimension_semantics=("arbitrary", "arbitrary"),
                                             vmem_limit_bytes=VMEM_LIMIT),
        name="post",
    )(*args)
    return out[0] if window is None else out


def _layer_weights(w_in, conv_w, a_log, dt_bias, g_dn_out, w_out, g_post_mix, g_pre_ffn, w_gate_up, w_down,
                   g_post_ffn, w_ple_proj, w_ple_gate, g_ple):
    c0 = CONV_CH
    c1 = c0 + DN_W
    c2 = c1 + 2 * DN_HEADS
    w_ba = jnp.pad(w_in[:, c1:c2], ((0, 0), (0, LANES - 2 * DN_HEADS)))
    w = jnp.concatenate([w_in[:, :c1], w_in[:, c2:], w_ba], axis=1).astype(BF16)
    assert w.shape[1] == IN_COLS_PAD
    ap = jnp.zeros((8, LANES), F32)
    ap = ap.at[0, DN_HEADS:2 * DN_HEADS].set(a_log.astype(F32))
    ap = ap.at[1, DN_HEADS:2 * DN_HEADS].set(dt_bias.astype(F32))
    gains = jnp.zeros((8, w_out.shape[1]), F32)
    for r, g in enumerate((g_post_mix, g_pre_ffn, g_post_ffn, g_ple)):
        gains = gains.at[r].set(g.astype(F32))
    return dict(w=w, cw=conv_w.astype(F32), ap=ap, gn=g_dn_out.astype(F32)[None], wo=w_out.astype(BF16),
                wgu=w_gate_up.astype(BF16), wd=w_down.astype(BF16), wpg=w_ple_gate.astype(BF16),
                wpp=w_ple_proj.astype(BF16), gains=gains)


def _layer(xp, xs, pp, ps, cache_conv, state, win_k, win_v, g_pre, lw, tables, stab, self_tab):
    B, S, D = xp.shape
    Bs, T, _ = xs.shape
    assert T == 1
    post_w = (lw["wo"], lw["wgu"], lw["wd"], lw["wpg"], lw["wpp"], lw["gains"])
    to_minor = lambda t: t.transpose(0, 2, 3, 1)
    from_minor = lambda t: t.transpose(0, 3, 1, 2)

    cols = lambda t: t.reshape(Bs, AT_HEADS, AT_HD).transpose(0, 2, 1)

    xs2 = xs.reshape(Bs, D)
    qt, kt, v_s, z_s, ba_s, qa_s, ka_s, va_s, nc_s = _inproj_sample(
        xs2, g_pre, lw["w"], lw["cw"], lw["ap"], cache_conv.transpose(1, 0, 2))
    o_dn_s, s_new = _delta_sample(qt, kt, v_s, z_s, ba_s, lw["gn"], state)

    (q, k, v, z, ba, qa, ka, va, wk, wv, nc, new_wk, s_sc, s_sc0) = _inproj_prompt(
        xp, g_pre, lw["w"], lw["cw"], lw["ap"], cols(qa_s), cols(ka_s), to_minor(win_k), stab, self_tab)
    o_dn, s_fin = _delta_prompt(q, k, v, z, ba, lw["gn"], tb=DELTA_TB)
    o_at = _attn_prompt(qa, ka, va, tables)
    yp, new_wv, o_at_s = _post(xp, o_dn, o_at, pp, *post_w, window=(s_sc, s_sc0, cols(va_s), to_minor(win_v)))

    o_at_s = o_at_s.transpose(0, 2, 1).reshape(1, Bs, AT_PAIRS, LANES).transpose(0, 2, 1, 3).astype(BF16)
    ys = _post(xs2[None], o_dn_s.reshape(1, Bs, DN_W).astype(BF16), o_at_s, ps.reshape(1, Bs, -1), *post_w)

    n_keep = wk.shape[2]
    return (yp, ys.reshape(Bs, 1, D),
            nc[:, 8 - (CONV_W - 1):], s_fin,
            from_minor(wk.reshape(B, AT_HEADS, AT_HD, n_keep)), from_minor(wv.reshape(B, AT_HEADS, AT_HD, n_keep)),
            nc_s.transpose(1, 0, 2), s_new, from_minor(new_wk), from_minor(new_wv))


def kernel(x_prompt, x_sample, cache_conv, state_delta, cache_win_k, cache_win_v, p_prompt, p_sample, rel_bias, g_pre_mix, w_in, conv_w, a_log, dt_bias, g_dn_out, w_out, g_post_mix, g_pre_ffn, w_gate_up, w_down, g_post_ffn, w_ple_proj, w_ple_gate, g_ple):
    depth = w_in.shape[0]
    tables = _band_tables(rel_bias)
    stab, self_tab = _sample_tables(rel_bias)
    yp, ys = x_prompt, x_sample
    outs = [[] for _ in range(8)]
    for i in range(depth):
        lw = _layer_weights(w_in[i], conv_w[i], a_log[i], dt_bias[i], g_dn_out[i], w_out[i], g_post_mix[i],
                            g_pre_ffn[i], w_gate_up[i], w_down[i], g_post_ffn[i], w_ple_proj[i], w_ple_gate[i],
                            g_ple[i])
        g_pre = g_pre_mix[i].astype(F32)[None]
        yp, ys, *rest = _layer(yp, ys, p_prompt[i], p_sample[i], cache_conv[i], state_delta[i], cache_win_k[i],
                               cache_win_v[i], g_pre, lw, tables, stab, self_tab)
        for lst, val in zip(outs, rest):
            lst.append(val)
    return (yp, ys) + tuple(jnp.stack(o) for o in outs)
```

```python
import functools
import math

import numpy as np
import jax
import jax.numpy as jnp
from jax import lax
from jax.experimental import pallas as pl
from jax.experimental.pallas import tpu as pltpu

DN_HEADS = 4
DN_D = 128
CONV_W = 4
AT_HEADS = 8
AT_HD = 64
AT_PAIRS = AT_HEADS // 2
DIL_PAIRS = ((128, 1), (512, 4), (2048, 16))
WIN_MAX = 2048
BAND = 128
NUM_BUCKETS = 32
MAX_DIST = 2048
RMS_EPS = 1e-6
NEG = -1e30
LOG2E = math.log2(math.e)

DN_W = DN_HEADS * DN_D
CONV_CH = 3 * DN_W
AT_W = AT_HEADS * AT_HD
LANES = 128
SUPER = 16 * BAND

V7X_VMEM_BYTES = 64 * 1024 * 1024
VMEM_LIMIT = V7X_VMEM_BYTES - 8 * 1024 * 1024

INPROJ_SUB = 256
DELTA_TB = 256
POST_TM = 512
POST_SUB = 128
FFN_CHUNK = 256

F32 = jnp.float32
BF16 = jnp.bfloat16


def _t5_bucket(n):
    max_exact = NUM_BUCKETS // 2
    n = np.asarray(n)
    large = max_exact + (np.log(np.maximum(n, 1) / max_exact) / math.log(MAX_DIST / max_exact)
                         * (NUM_BUCKETS - max_exact)).astype(np.int32)
    large = np.minimum(large, NUM_BUCKETS - 1)
    return np.where(n < max_exact, n, large).astype(np.int32)


def _sigmoid(x):
    return 1.0 / (1.0 + jnp.exp(-x))


def _silu(x):
    return x * _sigmoid(x)


def _softplus(x):
    return jnp.maximum(x, 0.0) + jnp.log1p(jnp.exp(-jnp.abs(x)))


def _rms(x, g):
    return x * lax.rsqrt(jnp.mean(x * x, axis=-1, keepdims=True) + RMS_EPS) * g


def _dot(a, b):
    return jnp.dot(a, b, preferred_element_type=F32)


def _dot_nt(a, b):
    return lax.dot_general(a, b, (((1,), (1,)), ((), ())), preferred_element_type=F32)


def _const_spec(shape):
    nd = len(shape)
    return pl.BlockSpec(shape, lambda *_: (0,) * nd, pipeline_mode=pl.Buffered(1))


C_CONV = 0
C_Z = CONV_CH
C_Q = C_Z + DN_W
C_K = C_Q + AT_W
C_V = C_K + AT_W
C_BA = C_V + AT_W
IN_COLS_PAD = C_BA + LANES


def _dn_activations(c, ba_raw, ap_ref):
    c = _silu(c)
    qs, ks = [], []
    for h in range(DN_HEADS):
        qh = c[:, h * DN_D:(h + 1) * DN_D]
        kh = c[:, DN_W + h * DN_D:DN_W + (h + 1) * DN_D]
        qs.append(qh * lax.rsqrt(jnp.sum(qh * qh, axis=-1, keepdims=True) + 1e-6) * (DN_D ** -0.5))
        ks.append(kh * lax.rsqrt(jnp.sum(kh * kh, axis=-1, keepdims=True) + 1e-6))
    q = jnp.concatenate(qs, axis=1)
    k = jnp.concatenate(ks, axis=1)
    v = c[:, 2 * DN_W:]
    lane = lax.broadcasted_iota(jnp.int32, ba_raw.shape, 1)
    beta = _sigmoid(ba_raw)
    gdec = -jnp.exp(ap_ref[0:1, :]) * _softplus(ba_raw + ap_ref[1:2, :])
    ba = jnp.where(lane < DN_HEADS, beta, gdec)
    return q, k, v, ba


def _inproj_prompt_kernel(x_ref, g_ref, w_ref, cw_ref, ap_ref, sq_ref, skn_ref, swin_ref, stab_ref, stab0_ref,
                          q_ref, k_ref, v_ref, z_ref, ba_ref, qa_ref, ka_ref, va_ref,
                          wk_ref, wv_ref, nc_ref, swin_out_ref, ss_ref, ss0_ref, buf_ref, rl_ref, *, tm):
    i = pl.program_id(1)

    @pl.when(i == 0)
    def _():
        buf_ref[0:8, :] = jnp.zeros((8, CONV_CH), F32)

    sub = min(tm, INPROJ_SUB)
    subs = [slice(lo, lo + sub) for lo in range(0, tm, sub)]
    hs = [_rms(x_ref[r, :], g_ref[...]).astype(BF16) for r in subs]
    us = [_dot(h, w_ref[:, C_CONV:C_Z]) for h in hs]
    for r, u in zip(subs, us):
        buf_ref[8 + r.start:8 + r.stop, :] = u
    bas = [_dot(h, w_ref[:, C_BA:C_BA + LANES]) for h in hs]
    ctx = CONV_W - 1
    for r, h, u, ba_raw in zip(subs, hs, us, bas):
        c = u * cw_ref[ctx:CONV_W, :]
        for t in range(ctx):
            c = c + buf_ref[8 - ctx + t + r.start:8 - ctx + t + r.stop, :] * cw_ref[t:t + 1, :]
        q, k, v, ba = _dn_activations(c, ba_raw, ap_ref)
        q_ref[r, :] = q.astype(BF16)
        k_ref[r, :] = k.astype(BF16)
        v_ref[r, :] = v.astype(BF16)
        ba_ref[r, :] = ba
        z_ref[r, :] = _dot(h, w_ref[:, C_Z:C_Q]).astype(BF16)

        qa = _dot(h, w_ref[:, C_Q:C_K]) * (AT_HD ** -0.5 * LOG2E)
        ka = _dot(h, w_ref[:, C_K:C_V])
        va = _dot(h, w_ref[:, C_V:C_BA])

        wk_ref[:, r] = ka.T
        wv_ref[:, r] = va.T

        for t_idx, (val, dst_ref) in enumerate(((qa, qa_ref), (ka, ka_ref), (va, va_ref))):
            for p in range(AT_PAIRS):
                slab = t_idx * AT_PAIRS + p
                for g in range(sub // 16):
                    m = (r.start + 16 * g) // 16
                    for half in range(2):
                        rows = slice(16 * g + 8 * half, 16 * g + 8 * half + 8)
                        rl_ref[slab, pl.ds(33 * m + 16 * half, 8, stride=2), :] = val[rows, p * LANES:(p + 1) * LANES]
    n_m = tm // 16
    for t_idx, dst_ref in enumerate((qa_ref, ka_ref, va_ref)):
        for p in range(AT_PAIRS):
            slab = t_idx * AT_PAIRS + p
            for c in range(16):
                piece = jnp.concatenate(
                    [rl_ref[slab, pl.ds(33 * 8 * blk + 2 * c, 8, stride=33), :] for blk in range(n_m // 8)], axis=0)
                dst_ref[p, :, c * LANES:(c + 1) * LANES] = piece.astype(BF16)
    last = us[-1][sub - 8:sub, :]
    buf_ref[0:8, :] = last
    nc_ref[...] = last

    _window_keys(sq_ref, skn_ref, swin_ref, stab_ref, stab0_ref, swin_out_ref, ss_ref, ss0_ref)


def _inproj_prompt(x, g, w, cw, ap, s_q, s_kn, s_win_k, s_tab, s_tab0):
    B, S, D = x.shape
    Bs, H, E, W = s_win_k.shape
    n_keep = min(WIN_MAX, S)
    tm = B * S // Bs
    assert B * S == Bs * tm and S % tm == 0 and n_keep % tm == 0 and tm % 128 == 0
    nt = S // tm
    first_keep = (S - n_keep) // tm
    col, slab, score, score0 = _window_specs(nt)
    tok = lambda width: pl.BlockSpec((None, tm, width), lambda b, i: (b, i, 0))
    pair = pl.BlockSpec((None, AT_PAIRS, tm // 16, 16 * LANES), lambda b, i: (b, 0, i, 0))
    keep = pl.BlockSpec((None, AT_W, tm), lambda b, i: (b, 0, jnp.maximum(i - first_keep, 0)))
    bf = lambda width: jax.ShapeDtypeStruct((B, S, width), BF16)
    pairs = jax.ShapeDtypeStruct((B, AT_PAIRS, S // 16, 16 * LANES), BF16)
    relayout_rows = -(-33 * (tm // 16) // 8) * 8
    return pl.pallas_call(
        functools.partial(_inproj_prompt_kernel, tm=tm),
        grid=(B, nt),
        in_specs=[tok(D), _const_spec((1, D)), _const_spec(w.shape), _const_spec(cw.shape),
                  _const_spec(ap.shape), col, col, slab, _const_spec(s_tab.shape), _const_spec(s_tab0.shape)],
        out_specs=[tok(DN_W), tok(DN_W), tok(DN_W), tok(DN_W), tok(LANES), pair, pair, pair,
                   keep, keep, pl.BlockSpec((None, 8, CONV_CH), lambda b, i: (b, 0, 0)),
                   slab, score, score0],
        out_shape=[bf(DN_W), bf(DN_W), bf(DN_W), bf(DN_W), jax.ShapeDtypeStruct((B, S, LANES), F32),
                   pairs, pairs, pairs,
                   jax.ShapeDtypeStruct((B, AT_W, n_keep), F32), jax.ShapeDtypeStruct((B, AT_W, n_keep), F32),
                   jax.ShapeDtypeStruct((B, 8, CONV_CH), F32),
                   jax.ShapeDtypeStruct(s_win_k.shape, F32), jax.ShapeDtypeStruct((Bs, H, W), F32),
                   jax.ShapeDtypeStruct((Bs, H, LANES), F32)],
        scratch_shapes=[pltpu.VMEM((tm + 8, CONV_CH), F32),
                        pltpu.VMEM((3 * AT_PAIRS, relayout_rows, LANES), F32)],
        compiler_params=pltpu.CompilerParams(dimension_semantics=("arbitrary", "arbitrary"),
                                             vmem_limit_bytes=VMEM_LIMIT),
        name="inproj_prompt",
    )(x, g, w, cw, ap, s_q, s_kn, s_win_k, s_tab, s_tab0)


def _inproj_sample_kernel(x_ref, g_ref, w_ref, cw_ref, ap_ref, cc_ref,
                          qt_ref, kt_ref, v_ref, z_ref, ba_ref, qa_ref, ka_ref, va_ref, nc_ref):
    x = x_ref[...]
    h = _rms(x, g_ref[...]).astype(BF16)
    u = _dot(h, w_ref[:, C_CONV:C_Z])
    c = u * cw_ref[CONV_W - 1:CONV_W, :]
    for t in range(CONV_W - 1):
        c = c + cc_ref[t] * cw_ref[t:t + 1, :]
    for t in range(1, CONV_W - 1):
        nc_ref[t - 1] = cc_ref[t]
    nc_ref[CONV_W - 2] = u

    ba_raw = _dot(h, w_ref[:, C_BA:C_BA + LANES])
    q, k, v, ba = _dn_activations(c, ba_raw, ap_ref)
    qt_ref[...] = q.T
    kt_ref[...] = k.T
    v_ref[...] = v
    ba_ref[...] = ba
    z_ref[...] = _dot(h, w_ref[:, C_Z:C_Q])
    qa_ref[...] = _dot(h, w_ref[:, C_Q:C_K]) * (AT_HD ** -0.5)
    ka_ref[...] = _dot(h, w_ref[:, C_K:C_V])
    va_ref[...] = _dot(h, w_ref[:, C_V:C_BA])


def _inproj_sample(x, g, w, cw, ap, conv_ctx):
    B, D = x.shape
    f = lambda *s: jax.ShapeDtypeStruct(s, F32)
    return pl.pallas_call(
        _inproj_sample_kernel,
        out_shape=[f(DN_W, B), f(DN_W, B), f(B, DN_W), f(B, DN_W), f(B, LANES),
                   f(B, AT_W), f(B, AT_W), f(B, AT_W), f(CONV_W - 1, B, CONV_CH)],
        compiler_params=pltpu.CompilerParams(vmem_limit_bytes=VMEM_LIMIT),
        name="inproj_sample",
    )(x, g, w, cw, ap, conv_ctx)


DN_CHUNK = 64


def _delta_prompt_kernel(q_ref, k_ref, v_ref, z_ref, ba_ref, gn_ref, o_ref, s_out_ref, s_ref, *, nb, tb):
    i = pl.program_id(0)
    C = DN_CHUNK
    nc = tb // C

    @pl.when(i == 0)
    def _():
        s_ref[...] = jnp.zeros(s_ref.shape, F32)

    rt = lax.broadcasted_iota(jnp.int32, (tb, tb), 0)
    ct = lax.broadcasted_iota(jnp.int32, (tb, tb), 1)
    tri = jnp.where((rt >= ct) & ((rt // C) == (ct // C)), 1.0, 0.0).astype(F32)
    ri = lax.broadcasted_iota(jnp.int32, (C, C), 0)
    ci = lax.broadcasted_iota(jnp.int32, (C, C), 1)
    eye = jnp.where(ri == ci, 1.0, 0.0).astype(F32)
    gn = gn_ref[...]

    units = [(b, h, c) for b in range(nb) for h in range(DN_HEADS) for c in range(nc)]
    pre = {}
    for b in range(nb):
        ba = ba_ref[b]
        gc_all = jnp.dot(tri, ba, preferred_element_type=F32, precision=lax.Precision.HIGHEST)
        pre[b] = (ba, gc_all, gc_all.T, jnp.exp(gc_all))
    kts = {(b, h): k_ref[b, :, h * DN_D:(h + 1) * DN_D].astype(F32).T
           for b in range(nb) for h in range(DN_HEADS)}

    st = {}
    for (b, h, c) in units:
        ba, gc_all, gc_t, eg_all = pre[b]
        rows = slice(c * C, (c + 1) * C)
        lanes = slice(h * DN_D, (h + 1) * DN_D)
        gl = DN_HEADS + h
        kf = k_ref[b, rows, lanes].astype(F32)
        qf = q_ref[b, rows, lanes].astype(F32)
        vf = v_ref[b, rows, lanes].astype(F32)
        beta = ba[rows, h:h + 1]
        gcol = gc_all[rows, gl:gl + 1]
        grow = gc_t[gl:gl + 1, rows]
        egc = eg_all[rows, gl:gl + 1]
        glast = gc_all[c * C + C - 1:c * C + C, gl:gl + 1]
        kb = kf * beta
        st[b, h, c] = dict(
            dec=jnp.exp(jnp.where(ri >= ci, gcol - grow, NEG)),
            lhs=jnp.concatenate([kb, qf], axis=0).astype(BF16),
            kbf=kf.astype(BF16),
            rhs=jnp.concatenate([vf * beta, kb * egc], axis=1).astype(BF16),
            qdec=(qf * egc).astype(BF16),
            kdt=(kts[b, h][:, rows] * jnp.exp(glast - grow)).astype(BF16),
            glast=jnp.exp(glast))
    for u in units:
        st[u]["both"] = _dot_nt(st[u]["lhs"], st[u]["kbf"])
    for u in units:
        d = st[u]
        a_mat = jnp.where(ri > ci, d["both"][:C] * d["dec"], 0.0)
        d["intra"] = (d["both"][C:] * d["dec"]).astype(BF16)
        d["nb"] = (-a_mat).astype(BF16)
        d["xm"] = eye - a_mat
    for u in units:
        st[u]["pm"] = _dot(st[u]["nb"], st[u]["nb"])
    for _ in range(4):
        for u in units:
            d = st[u]
            d["r"] = _dot(jnp.concatenate([d["xm"], d["pm"]], axis=0).astype(BF16), d["pm"].astype(BF16))
        for u in units:
            d = st[u]
            d["xm"] = d["xm"] + d["r"][:C]
            d["pm"] = d["r"][C:]
    for u in units:
        d = st[u]
        d["r"] = _dot(d["xm"].astype(BF16), d["pm"].astype(BF16))
    for u in units:
        d = st[u]
        d["uw"] = _dot((d["xm"] + d["r"]).astype(BF16), d["rhs"])

    chains = [(b, h) for b in range(nb) for h in range(DN_HEADS)]
    state = {bh: s_ref[bh[0], bh[1]] for bh in chains}
    for c in range(nc):
        rows = slice(c * C, (c + 1) * C)
        r2 = {}
        for (b, h) in chains:
            d = st[b, h, c]
            lhs = jnp.concatenate([d["uw"][:, DN_D:].astype(BF16), d["qdec"]], axis=0)
            r2[b, h] = _dot(lhs, state[b, h].astype(BF16))
        vb = {bh: (st[bh[0], bh[1], c]["uw"][:, :DN_D] - r2[bh][:C]).astype(BF16) for bh in chains}
        for (b, h) in chains:
            d = st[b, h, c]
            o = r2[b, h][C:] + _dot(d["intra"], vb[b, h])
            state[b, h] = state[b, h] * d["glast"] + _dot(d["kdt"], vb[b, h])
            lanes = slice(h * DN_D, (h + 1) * DN_D)
            zf = z_ref[b, rows, lanes].astype(F32)
            o_ref[b, rows, lanes] = (_rms(o, gn) * _silu(zf)).astype(BF16)
    for (b, h) in chains:
        s_ref[b, h] = state[b, h]

    @pl.when(i == pl.num_programs(0) - 1)
    def _():
        s_out_ref[...] = s_ref[...]


def _delta_prompt(q, k, v, z, ba, gn, *, tb):
    B, S, _ = q.shape
    assert S % tb == 0 and tb % DN_CHUNK == 0
    tok = lambda width: pl.BlockSpec((B, tb, width), lambda i: (0, i, 0))
    st = pl.BlockSpec((B, DN_HEADS, DN_D, DN_D), lambda i: (0, 0, 0, 0))
    return pl.pallas_call(
        functools.partial(_delta_prompt_kernel, nb=B, tb=tb),
        grid=(S // tb,),
        in_specs=[tok(DN_W), tok(DN_W), tok(DN_W), tok(DN_W), tok(LANES), _const_spec((1, DN_D))],
        out_specs=[tok(DN_W), st],
        out_shape=[jax.ShapeDtypeStruct((B, S, DN_W), BF16),
                   jax.ShapeDtypeStruct((B, DN_HEADS, DN_D, DN_D), F32)],
        scratch_shapes=[pltpu.VMEM((B, DN_HEADS, DN_D, DN_D), F32)],
        compiler_params=pltpu.CompilerParams(dimension_semantics=("arbitrary",),
                                             vmem_limit_bytes=VMEM_LIMIT),
        name="delta_prompt",
    )(q, k, v, z, ba, gn)


SAMPLE_SEQS_PER_STEP = 8


def _delta_sample_kernel(qt_ref, kt_ref, v_ref, z_ref, ba_ref, gn_ref, s_ref, o_ref, s_out_ref):
    step = pl.program_id(0)
    nb = qt_ref.shape[1]
    per = v_ref.shape[0]
    lane = lax.broadcasted_iota(jnp.int32, (DN_D, nb), 1)
    gn = gn_ref[...]
    units = [(j, h) for j in range(per) for h in range(DN_HEADS)]
    rows = {h: slice(h * DN_D, (h + 1) * DN_D) for h in range(DN_HEADS)}
    sel = {j: lane == step * per + j for j in range(per)}
    qcol = {(j, h): jnp.sum(jnp.where(sel[j], qt_ref[rows[h], :], 0.0), axis=1, keepdims=True)
            for (j, h) in units}
    kcol = {(j, h): jnp.sum(jnp.where(sel[j], kt_ref[rows[h], :], 0.0), axis=1, keepdims=True)
            for (j, h) in units}
    egd = jnp.exp(ba_ref[...])
    s = {(j, h): s_ref[j, h] * egd[j:j + 1, DN_HEADS + h:DN_HEADS + h + 1] for (j, h) in units}
    kv = {u: jnp.sum(kcol[u] * s[u], axis=0, keepdims=True) for u in units}
    for (j, h) in units:
        delta = (v_ref[j:j + 1, rows[h]] - kv[j, h]) * ba_ref[j:j + 1, h:h + 1]
        s[j, h] = s[j, h] + kcol[j, h] * delta
        s_out_ref[j, h] = s[j, h]
    o = {u: jnp.sum(qcol[u] * s[u], axis=0, keepdims=True) for u in units}
    for (j, h) in units:
        o_ref[j:j + 1, rows[h]] = _rms(o[j, h], gn) * _silu(z_ref[j:j + 1, rows[h]])


def _delta_sample(qt, kt, v, z, ba, gn, state):
    B = v.shape[0]
    per = SAMPLE_SEQS_PER_STEP
    assert B % per == 0
    row = lambda width: pl.BlockSpec((per, width), lambda b: (b, 0))
    st = pl.BlockSpec((per, DN_HEADS, DN_D, DN_D), lambda b: (b, 0, 0, 0))
    return pl.pallas_call(
        _delta_sample_kernel,
        grid=(B // per,),
        in_specs=[_const_spec(qt.shape), _const_spec(kt.shape), row(DN_W), row(DN_W), row(LANES),
                  _const_spec((1, DN_D)), st],
        out_specs=[row(DN_W), st],
        out_shape=[jax.ShapeDtypeStruct((B, DN_W), F32), jax.ShapeDtypeStruct(state.shape, F32)],
        compiler_params=pltpu.CompilerParams(dimension_semantics=("arbitrary",)),
        name="delta_sample",
    )(qt, kt, v, z, ba, gn, state)


def _stack_position(d):
    a = np.arange(BAND)
    rows = 8 * d
    return (16 // d) * (a % rows) + a // rows


def _bias_lookup(rel_bias, bucket):
    onehot = jnp.asarray(bucket[..., None] == np.arange(NUM_BUCKETS), dtype=BF16).astype(F32)
    return jnp.einsum("...n,nh->h...", onehot, rel_bias.astype(F32), precision=lax.Precision.HIGHEST)


def _band_tables(rel_bias):
    tables = []
    for w, d in DIL_PAIRS:
        assert w // d == BAND
        pos = _stack_position(d)
        dist = pos[:, None] - np.concatenate([pos - BAND, pos])[None, :]
        valid = (dist >= 0) & (dist <= BAND)
        bucket = _t5_bucket(np.clip(dist, 0, BAND) * d)
        bias = jnp.where(valid[None], _bias_lookup(rel_bias, bucket) * LOG2E, NEG)
        tables.append(bias.reshape(AT_PAIRS, 2 * BAND, 2 * BAND))
    return jnp.stack(tables)


def _attn_unit(q, kk, vv, table, pen):
    lane = lax.broadcasted_iota(jnp.int32, (1, LANES), 1)
    first = lane < AT_HD
    zero = jnp.zeros_like(q)
    q2 = jnp.concatenate([jnp.where(first, q, zero), jnp.where(first, zero, q)], axis=0)
    s = _dot_nt(q2, kk) + table
    if pen is not None:
        s = s + pen
    m = jnp.max(s, axis=1, keepdims=True)
    p = jnp.exp2(s - m)
    l = jnp.sum(p, axis=1, keepdims=True)
    pv = _dot(p.astype(BF16), vv)
    acc = jnp.where(first, pv[:BAND], pv[BAND:])
    mt = jnp.where(first, m[:BAND], m[BAND:])
    lt = jnp.where(first, l[:BAND], l[BAND:])
    return acc, mt, lt


def _attn_prompt_kernel(q_ref, k_ref, v_ref, t_ref, o_ref,
                        k16, v16, q4, k4, v4, q1, k1, v1, acc_ref, m_ref, l_ref):
    n = pl.program_id(2)

    @pl.when(n == 0)
    def _():
        zb = jnp.zeros((BAND, LANES), BF16)
        for rho in range(16):
            k16[rho, 0:BAND, :] = zb
            v16[rho, 0:BAND, :] = zb
        for rho in range(4):
            k4[rho, 0:BAND, :] = zb
            v4[rho, 0:BAND, :] = zb
        k1[0:BAND, :] = zb
        v1[0:BAND, :] = zb

    @pl.when(n > 0)
    def _():
        for rho in range(16):
            k16[rho, 0:BAND, :] = k16[rho, BAND:2 * BAND, :]
            v16[rho, 0:BAND, :] = v16[rho, BAND:2 * BAND, :]
        for rho in range(4):
            k4[rho, 0:BAND, :] = k4[rho, 4 * BAND:5 * BAND, :]
            v4[rho, 0:BAND, :] = v4[rho, 4 * BAND:5 * BAND, :]
        k1[0:BAND, :] = k1[SUPER:SUPER + BAND, :]
        v1[0:BAND, :] = v1[SUPER:SUPER + BAND, :]

    for rho in range(16):
        cols = slice(rho * LANES, (rho + 1) * LANES)
        k16[rho, BAND:2 * BAND, :] = k_ref[:, cols]
        v16[rho, BAND:2 * BAND, :] = v_ref[:, cols]
    for rho in range(4):
        for blk in range(4):
            for j in range(4):
                src = (slice(32 * blk, 32 * blk + 32), slice((4 * j + rho) * LANES, (4 * j + rho + 1) * LANES))
                dst = slice(BAND * blk + 32 * j, BAND * blk + 32 * j + 32)
                q4[rho, dst, :] = q_ref[src]
                k4[rho, BAND + dst.start:BAND + dst.stop, :] = k_ref[src]
                v4[rho, BAND + dst.start:BAND + dst.stop, :] = v_ref[src]
    for src_ref, dst_ref, off in ((q_ref, q1, 0), (k_ref, k1, BAND), (v_ref, v1, BAND)):
        for gam in range(8):
            lo, hi = [], []
            for r in range(16):
                piece = src_ref[16 * gam:16 * gam + 16, r * LANES:(r + 1) * LANES].astype(F32)
                lo.append(piece[0:8])
                hi.append(piece[8:16])
            dst_ref[off + BAND * (2 * gam):off + BAND * (2 * gam + 1), :] = jnp.concatenate(lo, axis=0).astype(BF16)
            dst_ref[off + BAND * (2 * gam + 1):off + BAND * (2 * gam + 2), :] = jnp.concatenate(hi, axis=0).astype(BF16)

    lane2 = lax.broadcasted_iota(jnp.int32, (1, 2 * BAND), 1)
    pen = jnp.where((lane2 < BAND) & (n == 0), NEG, 0.0).astype(F32)

    t16 = t_ref[2]
    for rho in range(16):
        cols = slice(rho * LANES, (rho + 1) * LANES)
        acc, mt, lt = _attn_unit(q_ref[:, cols], k16[rho], v16[rho], t16, pen)
        acc_ref[:, cols] = acc
        m_ref[:, cols] = mt
        l_ref[:, cols] = lt

    def merge(rows, cols, acc, mt, lt, final):
        m_old = m_ref[rows, cols]
        m_new = jnp.maximum(m_old, mt)
        a = jnp.exp2(m_old - m_new)
        b = jnp.exp2(mt - m_new)
        acc_new = acc_ref[rows, cols] * a + acc * b
        l_new = l_ref[rows, cols] * a + lt * b
        if final:
            acc_ref[rows, cols] = acc_new / l_new
        else:
            acc_ref[rows, cols] = acc_new
            m_ref[rows, cols] = m_new
            l_ref[rows, cols] = l_new

    t4 = t_ref[1]
    for rho in range(4):
        for blk in range(4):
            acc, mt, lt = _attn_unit(q4[rho, BAND * blk:BAND * (blk + 1), :],
                                     k4[rho, BAND * blk:BAND * (blk + 2), :],
                                     v4[rho, BAND * blk:BAND * (blk + 2), :],
                                     t4, pen if blk == 0 else None)
            for j in range(4):
                sub = slice(32 * j, 32 * j + 32)
                merge(slice(32 * blk, 32 * blk + 32), slice((4 * j + rho) * LANES, (4 * j + rho + 1) * LANES),
                      acc[sub], mt[sub], lt[sub], False)

    t1 = t_ref[0]
    for blk in range(16):
        acc, mt, lt = _attn_unit(q1[BAND * blk:BAND * (blk + 1), :],
                                 k1[BAND * blk:BAND * (blk + 2), :],
                                 v1[BAND * blk:BAND * (blk + 2), :],
                                 t1, pen if blk == 0 else None)
        for r in range(16):
            sub = slice(8 * r, 8 * r + 8)
            merge(slice(8 * blk, 8 * blk + 8), slice(r * LANES, (r + 1) * LANES),
                  acc[sub], mt[sub], lt[sub], True)

    o_ref[...] = acc_ref[...].astype(BF16)


def _attn_prompt(q, k, v, tables):
    B, P, S16, _ = q.shape
    S = 16 * S16
    assert S % SUPER == 0
    nt = S // SUPER
    tile = pl.BlockSpec((None, None, BAND, 16 * LANES), lambda b, p, n: (b, p, n, 0))
    tab = pl.BlockSpec((3, None, 2 * BAND, 2 * BAND), lambda b, p, n: (0, p, 0, 0))
    vm = lambda *s: pltpu.VMEM(s, BF16)
    out = pl.pallas_call(
        _attn_prompt_kernel,
        grid=(B, P, nt),
        in_specs=[tile, tile, tile, tab],
        out_specs=tile,
        out_shape=jax.ShapeDtypeStruct((B, P, S // 16, 16 * LANES), BF16),
        scratch_shapes=[vm(16, 2 * BAND, LANES), vm(16, 2 * BAND, LANES),
                        vm(4, 4 * BAND, LANES), vm(4, 5 * BAND, LANES), vm(4, 5 * BAND, LANES),
                        vm(SUPER, LANES), vm(SUPER + BAND, LANES), vm(SUPER + BAND, LANES),
                        pltpu.VMEM((BAND, 16 * LANES), F32), pltpu.VMEM((BAND, 16 * LANES), F32),
                        pltpu.VMEM((BAND, 16 * LANES), F32)],
        compiler_params=pltpu.CompilerParams(dimension_semantics=("arbitrary", "arbitrary", "arbitrary"),
                                             vmem_limit_bytes=VMEM_LIMIT),
        name="attn_prompt",
    )(q, k, v, tables)
    return out.reshape(B, P, S, LANES)


def _sample_tables(rel_bias):
    W = WIN_MAX
    j = W - np.arange(W)
    count = sum(((j % d == 0) & (j <= w)).astype(np.int64) for w, d in DIL_PAIRS)
    bias = _bias_lookup(rel_bias, _t5_bucket(j))
    tab = jnp.where(count[None] > 0, bias + np.log(np.maximum(count, 1))[None].astype(np.float32), NEG)
    self_tab = rel_bias[0].astype(F32) + math.log(len(DIL_PAIRS))
    return tab, jnp.broadcast_to(self_tab[:, None], (AT_HEADS, LANES))


def _window_keys(q_ref, kn_ref, k_ref, t_ref, t0_ref, ok_ref, s_ref, s0_ref):
    H, E, W = k_ref.shape
    last = lax.broadcasted_iota(jnp.int32, (1, W), 1) == W - 1
    for h in range(H):
        qc = q_ref[:, h:h + 1]
        knc = kn_ref[:, h:h + 1]
        kh = k_ref[h]
        s_ref[h:h + 1, :] = jnp.sum(kh * qc, axis=0, keepdims=True) + t_ref[h:h + 1, :]
        s0_ref[h:h + 1, :] = jnp.sum(qc * knc, axis=0, keepdims=True) + t0_ref[h:h + 1, :]
        ok_ref[h] = jnp.where(last, knc, pltpu.roll(kh, W - 1, axis=1))


def _window_values(s_ref, s0_ref, vn_ref, v_ref, ov_ref, o_ref):
    H, E, W = v_ref.shape
    last = lax.broadcasted_iota(jnp.int32, (1, W), 1) == W - 1
    s = s_ref[...]
    s0 = s0_ref[:, 0:1]
    m = jnp.maximum(jnp.max(s, axis=1, keepdims=True), s0)
    p = jnp.exp(s - m)
    p0 = jnp.exp(s0 - m)
    l = jnp.sum(p, axis=1, keepdims=True) + p0
    for h in range(H):
        vh = v_ref[h]
        vnc = vn_ref[:, h:h + 1]
        acc = jnp.sum(vh * p[h:h + 1, :], axis=1, keepdims=True) + p0[h:h + 1, :] * vnc
        o_ref[:, h:h + 1] = acc / l[h:h + 1, :]
        ov_ref[h] = jnp.where(last, vnc, pltpu.roll(vh, W - 1, axis=1))


def _window_specs(nt):
    seq = lambda b, i: b * nt + i
    col = pl.BlockSpec((None, AT_HD, AT_HEADS), lambda b, i: (seq(b, i), 0, 0))
    slab = pl.BlockSpec((None, AT_HEADS, AT_HD, WIN_MAX), lambda b, i: (seq(b, i), 0, 0, 0))
    score = pl.BlockSpec((None, AT_HEADS, WIN_MAX), lambda b, i: (seq(b, i), 0, 0))
    score0 = pl.BlockSpec((None, AT_HEADS, LANES), lambda b, i: (seq(b, i), 0, 0))
    return col, slab, score, score0


def _post_kernel(x_ref, od_ref, oa_ref, p_ref, wo_ref, wgu_ref, wd_ref, wpg_ref, wpp_ref, gains_ref, *rest):
    if len(rest) == 1:
        (y_ref,) = rest
    else:
        ss_ref, ss0_ref, svn_ref, swin_ref, y_ref, swin_out_ref, so_ref = rest
        _window_values(ss_ref, ss0_ref, svn_ref, swin_ref, swin_out_ref, so_ref)
    ffn = wd_ref.shape[0]
    tm = x_ref.shape[0]
    sub = min(tm, POST_SUB)
    subs = [slice(lo, lo + sub) for lo in range(0, tm, sub)]
    ns = len(subs)
    mix = [_dot(jnp.concatenate([od_ref[r, :]] + [oa_ref[p, r, :] for p in range(AT_PAIRS)], axis=1), wo_ref[...])
           for r in subs]
    x1 = [x_ref[r, :] + _rms(m, gains_ref[0:1, :]) for r, m in zip(subs, mix)]
    h2 = [_rms(t, gains_ref[1:2, :]).astype(BF16) for t in x1]

    work = [(lo, s) for lo in range(0, ffn, FFN_CHUNK) for s in range(ns)]
    gate_up = {}

    def issue(lo, s):
        gate_up[lo, s] = (_dot(h2[s], wgu_ref[:, lo:lo + FFN_CHUNK]),
                          _dot(h2[s], wgu_ref[:, ffn + lo:ffn + lo + FFN_CHUNK]))

    issue(*work[0])
    down = [None] * ns
    for idx, (lo, s) in enumerate(work):
        if idx + 1 < len(work):
            issue(*work[idx + 1])
        gt, up = gate_up.pop((lo, s))
        part = _dot((_silu(gt) * up).astype(BF16), wd_ref[lo:lo + FFN_CHUNK, :])
        down[s] = part if down[s] is None else down[s] + part

    x2 = [a + _rms(d, gains_ref[2:3, :]) for a, d in zip(x1, down)]
    gate = [_dot(t.astype(BF16), wpg_ref[...]) for t in x2]
    pe = [_dot(p_ref[r, :].astype(BF16), wpp_ref[...]) for r in subs]
    for r, t, g, e in zip(subs, x2, gate, pe):
        y_ref[r, :] = t + _rms(_sigmoid(g) * e, gains_ref[3:4, :])


def _post(x, o_dn, o_at, p, wo, wgu, wd, wpg, wpp, gains, window=None):
    B, S, D = x.shape
    tm = min(S, POST_TM) if window is None else B * S // window[3].shape[0]
    assert S % tm == 0
    nt = S // tm
    tok = lambda width: pl.BlockSpec((None, tm, width), lambda b, i: (b, i, 0))
    in_specs = [tok(D), tok(DN_W), pl.BlockSpec((None, AT_PAIRS, tm, LANES), lambda b, i: (b, 0, i, 0)),
                tok(p.shape[-1]),
                _const_spec(wo.shape), _const_spec(wgu.shape), _const_spec(wd.shape),
                _const_spec(wpg.shape), _const_spec(wpp.shape), _const_spec(gains.shape)]
    out_specs = [tok(D)]
    out_shape = [jax.ShapeDtypeStruct((B, S, D), F32)]
    args = [x, o_dn, o_at, p, wo, wgu, wd, wpg, wpp, gains]
    if window is not None:
        Bs, H, E, W = window[3].shape
        assert B * S == Bs * tm
        col, slab, score, score0 = _window_specs(nt)
        in_specs += [score, score0, col, slab]
        out_specs += [slab, col]
        out_shape += [jax.ShapeDtypeStruct((Bs, H, E, W), F32), jax.ShapeDtypeStruct((Bs, E, H), F32)]
        args += list(window)
    out = pl.pallas_call(
        _post_kernel,
        grid=(B, nt),
        in_specs=in_specs,
        out_specs=out_specs,
        out_shape=out_shape,
        compiler_params=pltpu.CompilerParams(dimension_semantics=("arbitrary", "arbitrary"),
                                             vmem_limit_bytes=VMEM_LIMIT),
        name="post",
    )(*args)
    return out[0] if window is None else out


def _layer_weights(w_in, conv_w, a_log, dt_bias, g_dn_out, w_out, g_post_mix, g_pre_ffn, w_gate_up, w_down,
                   g_post_ffn, w_ple_proj, w_ple_gate, g_ple):
    c0 = CONV_CH
    c1 = c0 + DN_W
    c2 = c1 + 2 * DN_HEADS
    w_ba = jnp.pad(w_in[:, c1:c2], ((0, 0), (0, LANES - 2 * DN_HEADS)))
    w = jnp.concatenate([w_in[:, :c1], w_in[:, c2:], w_ba], axis=1).astype(BF16)
    assert w.shape[1] == IN_COLS_PAD
    ap = jnp.zeros((8, LANES), F32)
    ap = ap.at[0, DN_HEADS:2 * DN_HEADS].set(a_log.astype(F32))
    ap = ap.at[1, DN_HEADS:2 * DN_HEADS].set(dt_bias.astype(F32))
    gains = jnp.zeros((8, w_out.shape[1]), F32)
    for r, g in enumerate((g_post_mix, g_pre_ffn, g_post_ffn, g_ple)):
        gains = gains.at[r].set(g.astype(F32))
    return dict(w=w, cw=conv_w.astype(F32), ap=ap, gn=g_dn_out.astype(F32)[None], wo=w_out.astype(BF16),
                wgu=w_gate_up.astype(BF16), wd=w_down.astype(BF16), wpg=w_ple_gate.astype(BF16),
                wpp=w_ple_proj.astype(BF16), gains=gains)


def _layer(xp, xs, pp, ps, cache_conv, state, win_k, win_v, g_pre, lw, tables, stab, self_tab):
    B, S, D = xp.shape
    Bs, T, _ = xs.shape
    assert T == 1
    post_w = (lw["wo"], lw["wgu"], lw["wd"], lw["wpg"], lw["wpp"], lw["gains"])
    to_minor = lambda t: t.transpose(0, 2, 3, 1)
    from_minor = lambda t: t.transpose(0, 3, 1, 2)

    cols = lambda t: t.reshape(Bs, AT_HEADS, AT_HD).transpose(0, 2, 1)

    xs2 = xs.reshape(Bs, D)
    qt, kt, v_s, z_s, ba_s, qa_s, ka_s, va_s, nc_s = _inproj_sample(
        xs2, g_pre, lw["w"], lw["cw"], lw["ap"], cache_conv.transpose(1, 0, 2))
    o_dn_s, s_new = _delta_sample(qt, kt, v_s, z_s, ba_s, lw["gn"], state)

    (q, k, v, z, ba, qa, ka, va, wk, wv, nc, new_wk, s_sc, s_sc0) = _inproj_prompt(
        xp, g_pre, lw["w"], lw["cw"], lw["ap"], cols(qa_s), cols(ka_s), to_minor(win_k), stab, self_tab)
    o_dn, s_fin = _delta_prompt(q, k, v, z, ba, lw["gn"], tb=DELTA_TB)
    o_at = _attn_prompt(qa, ka, va, tables)
    yp, new_wv, o_at_s = _post(xp, o_dn, o_at, pp, *post_w, window=(s_sc, s_sc0, cols(va_s), to_minor(win_v)))

    o_at_s = o_at_s.transpose(0, 2, 1).reshape(1, Bs, AT_PAIRS, LANES).transpose(0, 2, 1, 3).astype(BF16)
    ys = _post(xs2[None], o_dn_s.reshape(1, Bs, DN_W).astype(BF16), o_at_s, ps.reshape(1, Bs, -1), *post_w)

    n_keep = wk.shape[2]
    return (yp, ys.reshape(Bs, 1, D),
            nc[:, 8 - (CONV_W - 1):], s_fin,
            from_minor(wk.reshape(B, AT_HEADS, AT_HD, n_keep)), from_minor(wv.reshape(B, AT_HEADS, AT_HD, n_keep)),
            nc_s.transpose(1, 0, 2), s_new, from_minor(new_wk), from_minor(new_wv))


def kernel(x_prompt, x_sample, cache_conv, state_delta, cache_win_k, cache_win_v, p_prompt, p_sample, rel_bias, g_pre_mix, w_in, conv_w, a_log, dt_bias, g_dn_out, w_out, g_post_mix, g_pre_ffn, w_gate_up, w_down, g_post_ffn, w_ple_proj, w_ple_gate, g_ple):
    depth = w_in.shape[0]
    tables = _band_tables(rel_bias)
    stab, self_tab = _sample_tables(rel_bias)
    yp, ys = x_prompt, x_sample
    outs = [[] for _ in range(8)]
    for i in range(depth):
        lw = _layer_weights(w_in[i], conv_w[i], a_log[i], dt_bias[i], g_dn_out[i], w_out[i], g_post_mix[i],
                            g_pre_ffn[i], w_gate_up[i], w_down[i], g_post_ffn[i], w_ple_proj[i], w_ple_gate[i],
                            g_ple[i])
        g_pre = g_pre_mix[i].astype(F32)[None]
        yp, ys, *rest = _layer(yp, ys, p_prompt[i], p_sample[i], cache_conv[i], state_delta[i], cache_win_k[i],
                               cache_win_v[i], g_pre, lw, tables, stab, self_tab)
        for lst, val in zip(outs, rest):
            lst.append(val)
    return (yp, ys) + tuple(jnp.stack(o) for o in outs)
```

```python
import functools
import math

import numpy as np
import jax
import jax.numpy as jnp
from jax import lax
from jax.experimental import pallas as pl
from jax.experimental.pallas import tpu as pltpu

DN_HEADS = 4
DN_D = 128
CONV_W = 4
AT_HEADS = 8
AT_HD = 64
AT_PAIRS = AT_HEADS // 2
DIL_PAIRS = ((128, 1), (512, 4), (2048, 16))
WIN_MAX = 2048
BAND = 128
NUM_BUCKETS = 32
MAX_DIST = 2048
RMS_EPS = 1e-6
NEG = -1e30
LOG2E = math.log2(math.e)

DN_W = DN_HEADS * DN_D
CONV_CH = 3 * DN_W
AT_W = AT_HEADS * AT_HD
LANES = 128
SUPER = 16 * BAND

V7X_VMEM_BYTES = 64 * 1024 * 1024
VMEM_LIMIT = V7X_VMEM_BYTES - 8 * 1024 * 1024

INPROJ_SUB = 256
DELTA_TB = 256
POST_TM = 512
POST_SUB = 128
FFN_CHUNK = 256

F32 = jnp.float32
BF16 = jnp.bfloat16


def _t5_bucket(n):
    max_exact = NUM_BUCKETS // 2
    n = np.asarray(n)
    large = max_exact + (np.log(np.maximum(n, 1) / max_exact) / math.log(MAX_DIST / max_exact)
                         * (NUM_BUCKETS - max_exact)).astype(np.int32)
    large = np.minimum(large, NUM_BUCKETS - 1)
    return np.where(n < max_exact, n, large).astype(np.int32)


def _sigmoid(x):
    return 1.0 / (1.0 + jnp.exp(-x))


def _silu(x):
    return x * _sigmoid(x)


def _softplus(x):
    return jnp.maximum(x, 0.0) + jnp.log1p(jnp.exp(-jnp.abs(x)))


def _rms(x, g):
    return x * lax.rsqrt(jnp.mean(x * x, axis=-1, keepdims=True) + RMS_EPS) * g


def _dot(a, b):
    return jnp.dot(a, b, preferred_element_type=F32)


def _dot_nt(a, b):
    return lax.dot_general(a, b, (((1,), (1,)), ((), ())), preferred_element_type=F32)


def _const_spec(shape):
    nd = len(shape)
    return pl.BlockSpec(shape, lambda *_: (0,) * nd, pipeline_mode=pl.Buffered(1))


C_CONV = 0
C_Z = CONV_CH
C_Q = C_Z + DN_W
C_K = C_Q + AT_W
C_V = C_K + AT_W
C_BA = C_V + AT_W
IN_COLS_PAD = C_BA + LANES


def _dn_activations(c, ba_raw, ap_ref):
    c = _silu(c)
    qs, ks = [], []
    for h in range(DN_HEADS):
        qh = c[:, h * DN_D:(h + 1) * DN_D]
        kh = c[:, DN_W + h * DN_D:DN_W + (h + 1) * DN_D]
        qs.append(qh * lax.rsqrt(jnp.sum(qh * qh, axis=-1, keepdims=True) + 1e-6) * (DN_D ** -0.5))
        ks.append(kh * lax.rsqrt(jnp.sum(kh * kh, axis=-1, keepdims=True) + 1e-6))
    q = jnp.concatenate(qs, axis=1)
    k = jnp.concatenate(ks, axis=1)
    v = c[:, 2 * DN_W:]
    lane = lax.broadcasted_iota(jnp.int32, ba_raw.shape, 1)
    beta = _sigmoid(ba_raw)
    gdec = -jnp.exp(ap_ref[0:1, :]) * _softplus(ba_raw + ap_ref[1:2, :])
    ba = jnp.where(lane < DN_HEADS, beta, gdec)
    return q, k, v, ba


def _inproj_prompt_kernel(x_ref, g_ref, w_ref, cw_ref, ap_ref, sq_ref, skn_ref, swin_ref, stab_ref, stab0_ref,
                          dn_ref, ba_ref, qa_ref, ka_ref, va_ref,
                          wk_ref, wv_ref, nc_ref, swin_out_ref, ss_ref, ss0_ref, buf_ref, rl_ref, *, tm):
    i = pl.program_id(1)

    @pl.when(i == 0)
    def _():
        buf_ref[0:8, :] = jnp.zeros((8, CONV_CH), F32)

    sub = min(tm, INPROJ_SUB)
    subs = [slice(lo, lo + sub) for lo in range(0, tm, sub)]
    hs = [_rms(x_ref[r, :], g_ref[...]).astype(BF16) for r in subs]
    us = [_dot(h, w_ref[:, C_CONV:C_Z]) for h in hs]
    for r, u in zip(subs, us):
        buf_ref[8 + r.start:8 + r.stop, :] = u
    bas = [_dot(h, w_ref[:, C_BA:C_BA + LANES]) for h in hs]
    ctx = CONV_W - 1
    for r, h, u, ba_raw in zip(subs, hs, us, bas):
        c = u * cw_ref[ctx:CONV_W, :]
        for t in range(ctx):
            c = c + buf_ref[8 - ctx + t + r.start:8 - ctx + t + r.stop, :] * cw_ref[t:t + 1, :]
        q, k, v, ba = _dn_activations(c, ba_raw, ap_ref)
        dn_ref[r, 0:DN_W] = q.astype(BF16)
        dn_ref[r, DN_W:2 * DN_W] = k.astype(BF16)
        dn_ref[r, 2 * DN_W:3 * DN_W] = v.astype(BF16)
        ba_ref[r, :] = ba
        dn_ref[r, 3 * DN_W:4 * DN_W] = _dot(h, w_ref[:, C_Z:C_Q]).astype(BF16)

        qa = _dot(h, w_ref[:, C_Q:C_K]) * (AT_HD ** -0.5 * LOG2E)
        ka = _dot(h, w_ref[:, C_K:C_V])
        va = _dot(h, w_ref[:, C_V:C_BA])

        wk_ref[:, r] = ka.T
        wv_ref[:, r] = va.T

        for t_idx, (val, dst_ref) in enumerate(((qa, qa_ref), (ka, ka_ref), (va, va_ref))):
            for p in range(AT_PAIRS):
                slab = t_idx * AT_PAIRS + p
                for g in range(sub // 16):
                    m = (r.start + 16 * g) // 16
                    for half in range(2):
                        rows = slice(16 * g + 8 * half, 16 * g + 8 * half + 8)
                        rl_ref[slab, pl.ds(33 * m + 16 * half, 8, stride=2), :] = val[rows, p * LANES:(p + 1) * LANES]
    n_m = tm // 16
    for t_idx, dst_ref in enumerate((qa_ref, ka_ref, va_ref)):
        for p in range(AT_PAIRS):
            slab = t_idx * AT_PAIRS + p
            for c in range(16):
                piece = jnp.concatenate(
                    [rl_ref[slab, pl.ds(33 * 8 * blk + 2 * c, 8, stride=33), :] for blk in range(n_m // 8)], axis=0)
                dst_ref[p, :, c * LANES:(c + 1) * LANES] = piece.astype(BF16)
    last = us[-1][sub - 8:sub, :]
    buf_ref[0:8, :] = last
    nc_ref[...] = last

    _window_keys(sq_ref, skn_ref, swin_ref, stab_ref, stab0_ref, swin_out_ref, ss_ref, ss0_ref)


def _inproj_prompt(x, g, w, cw, ap, s_q, s_kn, s_win_k, s_tab, s_tab0):
    B, S, D = x.shape
    Bs, H, E, W = s_win_k.shape
    n_keep = min(WIN_MAX, S)
    tm = B * S // Bs
    assert B * S == Bs * tm and S % tm == 0 and n_keep % tm == 0 and tm % 128 == 0
    nt = S // tm
    first_keep = (S - n_keep) // tm
    col, slab, score, score0 = _window_specs(nt)
    tok = lambda width: pl.BlockSpec((None, tm, width), lambda b, i: (b, i, 0))
    pair = pl.BlockSpec((None, AT_PAIRS, tm // 16, 16 * LANES), lambda b, i: (b, 0, i, 0))
    keep = pl.BlockSpec((None, AT_W, tm), lambda b, i: (b, 0, jnp.maximum(i - first_keep, 0)))
    bf = lambda width: jax.ShapeDtypeStruct((B, S, width), BF16)
    pairs = jax.ShapeDtypeStruct((B, AT_PAIRS, S // 16, 16 * LANES), BF16)
    relayout_rows = -(-33 * (tm // 16) // 8) * 8
    return pl.pallas_call(
        functools.partial(_inproj_prompt_kernel, tm=tm),
        grid=(B, nt),
        in_specs=[tok(D), _const_spec((1, D)), _const_spec(w.shape), _const_spec(cw.shape),
                  _const_spec(ap.shape), col, col, slab, _const_spec(s_tab.shape), _const_spec(s_tab0.shape)],
        out_specs=[tok(4 * DN_W), tok(LANES), pair, pair, pair,
                   keep, keep, pl.BlockSpec((None, 8, CONV_CH), lambda b, i: (b, 0, 0)),
                   slab, score, score0],
        out_shape=[bf(4 * DN_W), jax.ShapeDtypeStruct((B, S, LANES), F32),
                   pairs, pairs, pairs,
                   jax.ShapeDtypeStruct((B, AT_W, n_keep), F32), jax.ShapeDtypeStruct((B, AT_W, n_keep), F32),
                   jax.ShapeDtypeStruct((B, 8, CONV_CH), F32),
                   jax.ShapeDtypeStruct(s_win_k.shape, F32), jax.ShapeDtypeStruct((Bs, H, W), F32),
                   jax.ShapeDtypeStruct((Bs, H, LANES), F32)],
        scratch_shapes=[pltpu.VMEM((tm + 8, CONV_CH), F32),
                        pltpu.VMEM((3 * AT_PAIRS, relayout_rows, LANES), F32)],
        compiler_params=pltpu.CompilerParams(dimension_semantics=("arbitrary", "arbitrary"),
                                             vmem_limit_bytes=VMEM_LIMIT),
        name="inproj_prompt",
    )(x, g, w, cw, ap, s_q, s_kn, s_win_k, s_tab, s_tab0)


def _inproj_sample_kernel(x_ref, g_ref, w_ref, cw_ref, ap_ref, cc_ref,
                          qt_ref, kt_ref, v_ref, z_ref, ba_ref, qa_ref, ka_ref, va_ref, nc_ref):
    x = x_ref[...]
    h = _rms(x, g_ref[...]).astype(BF16)
    u = _dot(h, w_ref[:, C_CONV:C_Z])
    c = u * cw_ref[CONV_W - 1:CONV_W, :]
    for t in range(CONV_W - 1):
        c = c + cc_ref[t] * cw_ref[t:t + 1, :]
    for t in range(1, CONV_W - 1):
        nc_ref[t - 1] = cc_ref[t]
    nc_ref[CONV_W - 2] = u

    ba_raw = _dot(h, w_ref[:, C_BA:C_BA + LANES])
    q, k, v, ba = _dn_activations(c, ba_raw, ap_ref)
    qt_ref[...] = q.T
    kt_ref[...] = k.T
    v_ref[...] = v
    ba_ref[...] = ba
    z_ref[...] = _dot(h, w_ref[:, C_Z:C_Q])
    qa_ref[...] = _dot(h, w_ref[:, C_Q:C_K]) * (AT_HD ** -0.5)
    ka_ref[...] = _dot(h, w_ref[:, C_K:C_V])
    va_ref[...] = _dot(h, w_ref[:, C_V:C_BA])


def _inproj_sample(x, g, w, cw, ap, conv_ctx):
    B, D = x.shape
    f = lambda *s: jax.ShapeDtypeStruct(s, F32)
    return pl.pallas_call(
        _inproj_sample_kernel,
        out_shape=[f(DN_W, B), f(DN_W, B), f(B, DN_W), f(B, DN_W), f(B, LANES),
                   f(B, AT_W), f(B, AT_W), f(B, AT_W), f(CONV_W - 1, B, CONV_CH)],
        compiler_params=pltpu.CompilerParams(vmem_limit_bytes=VMEM_LIMIT),
        name="inproj_sample",
    )(x, g, w, cw, ap, conv_ctx)


DN_CHUNK = 64


def _delta_prompt_kernel(dn_ref, ba_ref, gn_ref, o_ref, s_out_ref, s_ref, *, nb, tb):
    i = pl.program_id(0)
    K0, V0, Z0 = DN_W, 2 * DN_W, 3 * DN_W
    C = DN_CHUNK
    nc = tb // C

    @pl.when(i == 0)
    def _():
        s_ref[...] = jnp.zeros(s_ref.shape, F32)

    rt = lax.broadcasted_iota(jnp.int32, (tb, tb), 0)
    ct = lax.broadcasted_iota(jnp.int32, (tb, tb), 1)
    tri = jnp.where((rt >= ct) & ((rt // C) == (ct // C)), 1.0, 0.0).astype(F32)
    ri = lax.broadcasted_iota(jnp.int32, (C, C), 0)
    ci = lax.broadcasted_iota(jnp.int32, (C, C), 1)
    eye = jnp.where(ri == ci, 1.0, 0.0).astype(F32)
    gn = gn_ref[...]

    units = [(b, h, c) for b in range(nb) for h in range(DN_HEADS) for c in range(nc)]
    pre = {}
    for b in range(nb):
        ba = ba_ref[b]
        gc_all = jnp.dot(tri, ba, preferred_element_type=F32, precision=lax.Precision.HIGHEST)
        pre[b] = (ba, gc_all, gc_all.T, jnp.exp(gc_all))
    kts = {(b, h): dn_ref[b, :, K0 + h * DN_D:K0 + (h + 1) * DN_D].astype(F32).T
           for b in range(nb) for h in range(DN_HEADS)}

    st = {}
    for (b, h, c) in units:
        ba, gc_all, gc_t, eg_all = pre[b]
        rows = slice(c * C, (c + 1) * C)
        lanes = slice(h * DN_D, (h + 1) * DN_D)
        gl = DN_HEADS + h
        kf = dn_ref[b, rows, K0 + h * DN_D:K0 + (h + 1) * DN_D].astype(F32)
        qf = dn_ref[b, rows, lanes].astype(F32)
        vf = dn_ref[b, rows, V0 + h * DN_D:V0 + (h + 1) * DN_D].astype(F32)
        beta = ba[rows, h:h + 1]
        gcol = gc_all[rows, gl:gl + 1]
        grow = gc_t[gl:gl + 1, rows]
        egc = eg_all[rows, gl:gl + 1]
        glast = gc_all[c * C + C - 1:c * C + C, gl:gl + 1]
        kb = kf * beta
        st[b, h, c] = dict(
            dec=jnp.exp(jnp.where(ri >= ci, gcol - grow, NEG)),
            lhs=jnp.concatenate([kb, qf], axis=0).astype(BF16),
            kbf=kf.astype(BF16),
            rhs=jnp.concatenate([vf * beta, kb * egc], axis=1).astype(BF16),
            qdec=(qf * egc).astype(BF16),
            kdt=(kts[b, h][:, rows] * jnp.exp(glast - grow)).astype(BF16),
            glast=jnp.exp(glast))
    for u in units:
        st[u]["both"] = _dot_nt(st[u]["lhs"], st[u]["kbf"])
    for u in units:
        d = st[u]
        a_mat = jnp.where(ri > ci, d["both"][:C] * d["dec"], 0.0)
        d["intra"] = (d["both"][C:] * d["dec"]).astype(BF16)
        d["nb"] = (-a_mat).astype(BF16)
        d["xm"] = eye - a_mat
    for u in units:
        st[u]["pm"] = _dot(st[u]["nb"], st[u]["nb"])
    for _ in range(4):
        for u in units:
            d = st[u]
            d["r"] = _dot(jnp.concatenate([d["xm"], d["pm"]], axis=0).astype(BF16), d["pm"].astype(BF16))
        for u in units:
            d = st[u]
            d["xm"] = d["xm"] + d["r"][:C]
            d["pm"] = d["r"][C:]
    for u in units:
        d = st[u]
        d["r"] = _dot(d["xm"].astype(BF16), d["pm"].astype(BF16))
    for u in units:
        d = st[u]
        d["uw"] = _dot((d["xm"] + d["r"]).astype(BF16), d["rhs"])

    chains = [(b, h) for b in range(nb) for h in range(DN_HEADS)]
    state = {bh: s_ref[bh[0], bh[1]] for bh in chains}
    for c in range(nc):
        rows = slice(c * C, (c + 1) * C)
        r2 = {}
        for (b, h) in chains:
            d = st[b, h, c]
            lhs = jnp.concatenate([d["uw"][:, DN_D:].astype(BF16), d["qdec"]], axis=0)
            r2[b, h] = _dot(lhs, state[b, h].astype(BF16))
        vb = {bh: (st[bh[0], bh[1], c]["uw"][:, :DN_D] - r2[bh][:C]).astype(BF16) for bh in chains}
        for (b, h) in chains:
            d = st[b, h, c]
            o = r2[b, h][C:] + _dot(d["intra"], vb[b, h])
            state[b, h] = state[b, h] * d["glast"] + _dot(d["kdt"], vb[b, h])
            lanes = slice(h * DN_D, (h + 1) * DN_D)
            zf = dn_ref[b, rows, Z0 + h * DN_D:Z0 + (h + 1) * DN_D].astype(F32)
            o_ref[b, rows, lanes] = (_rms(o, gn) * _silu(zf)).astype(BF16)
    for (b, h) in chains:
        s_ref[b, h] = state[b, h]

    @pl.when(i == pl.num_programs(0) - 1)
    def _():
        s_out_ref[...] = s_ref[...]


def _delta_prompt(dn, ba, gn, *, tb):
    B, S, _ = dn.shape
    assert S % tb == 0 and tb % DN_CHUNK == 0
    tok = lambda width: pl.BlockSpec((B, tb, width), lambda i: (0, i, 0))
    st = pl.BlockSpec((B, DN_HEADS, DN_D, DN_D), lambda i: (0, 0, 0, 0))
    return pl.pallas_call(
        functools.partial(_delta_prompt_kernel, nb=B, tb=tb),
        grid=(S // tb,),
        in_specs=[tok(4 * DN_W), tok(LANES), _const_spec((1, DN_D))],
        out_specs=[tok(DN_W), st],
        out_shape=[jax.ShapeDtypeStruct((B, S, DN_W), BF16),
                   jax.ShapeDtypeStruct((B, DN_HEADS, DN_D, DN_D), F32)],
        scratch_shapes=[pltpu.VMEM((B, DN_HEADS, DN_D, DN_D), F32)],
        compiler_params=pltpu.CompilerParams(dimension_semantics=("arbitrary",),
                                             vmem_limit_bytes=VMEM_LIMIT),
        name="delta_prompt",
    )(dn, ba, gn)


SAMPLE_SEQS_PER_STEP = 8


def _delta_sample_kernel(qt_ref, kt_ref, v_ref, z_ref, ba_ref, gn_ref, s_ref, o_ref, s_out_ref):
    step = pl.program_id(0)
    nb = qt_ref.shape[1]
    per = v_ref.shape[0]
    lane = lax.broadcasted_iota(jnp.int32, (DN_D, nb), 1)
    gn = gn_ref[...]
    units = [(j, h) for j in range(per) for h in range(DN_HEADS)]
    rows = {h: slice(h * DN_D, (h + 1) * DN_D) for h in range(DN_HEADS)}
    sel = {j: lane == step * per + j for j in range(per)}
    qcol = {(j, h): jnp.sum(jnp.where(sel[j], qt_ref[rows[h], :], 0.0), axis=1, keepdims=True)
            for (j, h) in units}
    kcol = {(j, h): jnp.sum(jnp.where(sel[j], kt_ref[rows[h], :], 0.0), axis=1, keepdims=True)
            for (j, h) in units}
    egd = jnp.exp(ba_ref[...])
    s = {(j, h): s_ref[j, h] * egd[j:j + 1, DN_HEADS + h:DN_HEADS + h + 1] for (j, h) in units}
    kv = {u: jnp.sum(kcol[u] * s[u], axis=0, keepdims=True) for u in units}
    for (j, h) in units:
        delta = (v_ref[j:j + 1, rows[h]] - kv[j, h]) * ba_ref[j:j + 1, h:h + 1]
        s[j, h] = s[j, h] + kcol[j, h] * delta
        s_out_ref[j, h] = s[j, h]
    o = {u: jnp.sum(qcol[u] * s[u], axis=0, keepdims=True) for u in units}
    for (j, h) in units:
        o_ref[j:j + 1, rows[h]] = _rms(o[j, h], gn) * _silu(z_ref[j:j + 1, rows[h]])


def _delta_sample(qt, kt, v, z, ba, gn, state):
    B = v.shape[0]
    per = SAMPLE_SEQS_PER_STEP
    assert B % per == 0
    row = lambda width: pl.BlockSpec((per, width), lambda b: (b, 0))
    st = pl.BlockSpec((per, DN_HEADS, DN_D, DN_D), lambda b: (b, 0, 0, 0))
    return pl.pallas_call(
        _delta_sample_kernel,
        grid=(B // per,),
        in_specs=[_const_spec(qt.shape), _const_spec(kt.shape), row(DN_W), row(DN_W), row(LANES),
                  _const_spec((1, DN_D)), st],
        out_specs=[row(DN_W), st],
        out_shape=[jax.ShapeDtypeStruct((B, DN_W), F32), jax.ShapeDtypeStruct(state.shape, F32)],
        compiler_params=pltpu.CompilerParams(dimension_semantics=("arbitrary",)),
        name="delta_sample",
    )(qt, kt, v, z, ba, gn, state)


def _stack_position(d):
    a = np.arange(BAND)
    rows = 8 * d
    return (16 // d) * (a % rows) + a // rows


def _bias_lookup(rel_bias, bucket):
    onehot = jnp.asarray(bucket[..., None] == np.arange(NUM_BUCKETS), dtype=BF16).astype(F32)
    return jnp.einsum("...n,nh->h...", onehot, rel_bias.astype(F32), precision=lax.Precision.HIGHEST)


def _band_tables(rel_bias):
    tables = []
    for w, d in DIL_PAIRS:
        assert w // d == BAND
        pos = _stack_position(d)
        dist = pos[:, None] - np.concatenate([pos - BAND, pos])[None, :]
        valid = (dist >= 0) & (dist <= BAND)
        bucket = _t5_bucket(np.clip(dist, 0, BAND) * d)
        bias = jnp.where(valid[None], _bias_lookup(rel_bias, bucket) * LOG2E, NEG)
        tables.append(bias.reshape(AT_PAIRS, 2 * BAND, 2 * BAND))
    return jnp.stack(tables)


def _attn_unit(q, kk, vv, table, pen):
    lane = lax.broadcasted_iota(jnp.int32, (1, LANES), 1)
    first = lane < AT_HD
    zero = jnp.zeros_like(q)
    q2 = jnp.concatenate([jnp.where(first, q, zero), jnp.where(first, zero, q)], axis=0)
    s = _dot_nt(q2, kk) + table
    if pen is not None:
        s = s + pen
    m = jnp.max(s, axis=1, keepdims=True)
    p = jnp.exp2(s - m)
    l = jnp.sum(p, axis=1, keepdims=True)
    pv = _dot(p.astype(BF16), vv)
    acc = jnp.where(first, pv[:BAND], pv[BAND:])
    mt = jnp.where(first, m[:BAND], m[BAND:])
    lt = jnp.where(first, l[:BAND], l[BAND:])
    return acc, mt, lt


def _attn_prompt_kernel(q_ref, k_ref, v_ref, t_ref, o_ref,
                        k16, v16, q4, k4, v4, q1, k1, v1, acc_ref, m_ref, l_ref):
    n = pl.program_id(2)

    @pl.when(n == 0)
    def _():
        zb = jnp.zeros((BAND, LANES), BF16)
        for rho in range(16):
            k16[rho, 0:BAND, :] = zb
            v16[rho, 0:BAND, :] = zb
        for rho in range(4):
            k4[rho, 0:BAND, :] = zb
            v4[rho, 0:BAND, :] = zb
        k1[0:BAND, :] = zb
        v1[0:BAND, :] = zb

    @pl.when(n > 0)
    def _():
        for rho in range(16):
            k16[rho, 0:BAND, :] = k16[rho, BAND:2 * BAND, :]
            v16[rho, 0:BAND, :] = v16[rho, BAND:2 * BAND, :]
        for rho in range(4):
            k4[rho, 0:BAND, :] = k4[rho, 4 * BAND:5 * BAND, :]
            v4[rho, 0:BAND, :] = v4[rho, 4 * BAND:5 * BAND, :]
        k1[0:BAND, :] = k1[SUPER:SUPER + BAND, :]
        v1[0:BAND, :] = v1[SUPER:SUPER + BAND, :]

    for rho in range(16):
        cols = slice(rho * LANES, (rho + 1) * LANES)
        k16[rho, BAND:2 * BAND, :] = k_ref[:, cols]
        v16[rho, BAND:2 * BAND, :] = v_ref[:, cols]
    for rho in range(4):
        for blk in range(4):
            for j in range(4):
                src = (slice(32 * blk, 32 * blk + 32), slice((4 * j + rho) * LANES, (4 * j + rho + 1) * LANES))
                dst = slice(BAND * blk + 32 * j, BAND * blk + 32 * j + 32)
                q4[rho, dst, :] = q_ref[src]
                k4[rho, BAND + dst.start:BAND + dst.stop, :] = k_ref[src]
                v4[rho, BAND + dst.start:BAND + dst.stop, :] = v_ref[src]
    for src_ref, dst_ref, off in ((q_ref, q1, 0), (k_ref, k1, BAND), (v_ref, v1, BAND)):
        for gam in range(8):
            lo, hi = [], []
            for r in range(16):
                piece = src_ref[16 * gam:16 * gam + 16, r * LANES:(r + 1) * LANES].astype(F32)
                lo.append(piece[0:8])
                hi.append(piece[8:16])
            dst_ref[off + BAND * (2 * gam):off + BAND * (2 * gam + 1), :] = jnp.concatenate(lo, axis=0).astype(BF16)
            dst_ref[off + BAND * (2 * gam + 1):off + BAND * (2 * gam + 2), :] = jnp.concatenate(hi, axis=0).astype(BF16)

    lane2 = lax.broadcasted_iota(jnp.int32, (1, 2 * BAND), 1)
    pen = jnp.where((lane2 < BAND) & (n == 0), NEG, 0.0).astype(F32)

    t16 = t_ref[2]
    for rho in range(16):
        cols = slice(rho * LANES, (rho + 1) * LANES)
        acc, mt, lt = _attn_unit(q_ref[:, cols], k16[rho], v16[rho], t16, pen)
        acc_ref[:, cols] = acc
        m_ref[:, cols] = mt
        l_ref[:, cols] = lt

    def merge(rows, cols, acc, mt, lt, final):
        m_old = m_ref[rows, cols]
        m_new = jnp.maximum(m_old, mt)
        a = jnp.exp2(m_old - m_new)
        b = jnp.exp2(mt - m_new)
        acc_new = acc_ref[rows, cols] * a + acc * b
        l_new = l_ref[rows, cols] * a + lt * b
        if final:
            acc_ref[rows, cols] = acc_new / l_new
        else:
            acc_ref[rows, cols] = acc_new
            m_ref[rows, cols] = m_new
            l_ref[rows, cols] = l_new

    t4 = t_ref[1]
    for rho in range(4):
        for blk in range(4):
            acc, mt, lt = _attn_unit(q4[rho, BAND * blk:BAND * (blk + 1), :],
                                     k4[rho, BAND * blk:BAND * (blk + 2), :],
                                     v4[rho, BAND * blk:BAND * (blk + 2), :],
                                     t4, pen if blk == 0 else None)
            for j in range(4):
                sub = slice(32 * j, 32 * j + 32)
                merge(slice(32 * blk, 32 * blk + 32), slice((4 * j + rho) * LANES, (4 * j + rho + 1) * LANES),
                      acc[sub], mt[sub], lt[sub], False)

    t1 = t_ref[0]
    for blk in range(16):
        acc, mt, lt = _attn_unit(q1[BAND * blk:BAND * (blk + 1), :],
                                 k1[BAND * blk:BAND * (blk + 2), :],
                                 v1[BAND * blk:BAND * (blk + 2), :],
                                 t1, pen if blk == 0 else None)
        for r in range(16):
            sub = slice(8 * r, 8 * r + 8)
            merge(slice(8 * blk, 8 * blk + 8), slice(r * LANES, (r + 1) * LANES),
                  acc[sub], mt[sub], lt[sub], True)

    o_ref[...] = acc_ref[...].astype(BF16)


def _attn_prompt(q, k, v, tables):
    B, P, S16, _ = q.shape
    S = 16 * S16
    assert S % SUPER == 0
    nt = S // SUPER
    tile = pl.BlockSpec((None, None, BAND, 16 * LANES), lambda b, p, n: (b, p, n, 0))
    tab = pl.BlockSpec((3, None, 2 * BAND, 2 * BAND), lambda b, p, n: (0, p, 0, 0))
    vm = lambda *s: pltpu.VMEM(s, BF16)
    out = pl.pallas_call(
        _attn_prompt_kernel,
        grid=(B, P, nt),
        in_specs=[tile, tile, tile, tab],
        out_specs=tile,
        out_shape=jax.ShapeDtypeStruct((B, P, S // 16, 16 * LANES), BF16),
        scratch_shapes=[vm(16, 2 * BAND, LANES), vm(16, 2 * BAND, LANES),
                        vm(4, 4 * BAND, LANES), vm(4, 5 * BAND, LANES), vm(4, 5 * BAND, LANES),
                        vm(SUPER, LANES), vm(SUPER + BAND, LANES), vm(SUPER + BAND, LANES),
                        pltpu.VMEM((BAND, 16 * LANES), F32), pltpu.VMEM((BAND, 16 * LANES), F32),
                        pltpu.VMEM((BAND, 16 * LANES), F32)],
        compiler_params=pltpu.CompilerParams(dimension_semantics=("arbitrary", "arbitrary", "arbitrary"),
                                             vmem_limit_bytes=VMEM_LIMIT),
        name="attn_prompt",
    )(q, k, v, tables)
    return out


def _sample_tables(rel_bias):
    W = WIN_MAX
    j = W - np.arange(W)
    count = sum(((j % d == 0) & (j <= w)).astype(np.int64) for w, d in DIL_PAIRS)
    bias = _bias_lookup(rel_bias, _t5_bucket(j))
    tab = jnp.where(count[None] > 0, bias + np.log(np.maximum(count, 1))[None].astype(np.float32), NEG)
    self_tab = rel_bias[0].astype(F32) + math.log(len(DIL_PAIRS))
    return tab, jnp.broadcast_to(self_tab[:, None], (AT_HEADS, LANES))


def _window_keys(q_ref, kn_ref, k_ref, t_ref, t0_ref, ok_ref, s_ref, s0_ref):
    H, E, W = k_ref.shape
    last = lax.broadcasted_iota(jnp.int32, (1, W), 1) == W - 1
    for h in range(H):
        qc = q_ref[:, h:h + 1]
        knc = kn_ref[:, h:h + 1]
        kh = k_ref[h]
        s_ref[h:h + 1, :] = jnp.sum(kh * qc, axis=0, keepdims=True) + t_ref[h:h + 1, :]
        s0_ref[h:h + 1, :] = jnp.sum(qc * knc, axis=0, keepdims=True) + t0_ref[h:h + 1, :]
        ok_ref[h] = jnp.where(last, knc, pltpu.roll(kh, W - 1, axis=1))


def _window_values(s_ref, s0_ref, vn_ref, v_ref, ov_ref, o_ref):
    H, E, W = v_ref.shape
    last = lax.broadcasted_iota(jnp.int32, (1, W), 1) == W - 1
    s = s_ref[...]
    s0 = s0_ref[:, 0:1]
    m = jnp.maximum(jnp.max(s, axis=1, keepdims=True), s0)
    p = jnp.exp(s - m)
    p0 = jnp.exp(s0 - m)
    l = jnp.sum(p, axis=1, keepdims=True) + p0
    for h in range(H):
        vh = v_ref[h]
        vnc = vn_ref[:, h:h + 1]
        acc = jnp.sum(vh * p[h:h + 1, :], axis=1, keepdims=True) + p0[h:h + 1, :] * vnc
        o_ref[:, h:h + 1] = acc / l[h:h + 1, :]
        ov_ref[h] = jnp.where(last, vnc, pltpu.roll(vh, W - 1, axis=1))


def _window_specs(nt):
    seq = lambda b, i: b * nt + i
    col = pl.BlockSpec((None, AT_HD, AT_HEADS), lambda b, i: (seq(b, i), 0, 0))
    slab = pl.BlockSpec((None, AT_HEADS, AT_HD, WIN_MAX), lambda b, i: (seq(b, i), 0, 0, 0))
    score = pl.BlockSpec((None, AT_HEADS, WIN_MAX), lambda b, i: (seq(b, i), 0, 0))
    score0 = pl.BlockSpec((None, AT_HEADS, LANES), lambda b, i: (seq(b, i), 0, 0))
    return col, slab, score, score0


def _post_kernel(x_ref, od_ref, oa_ref, p_ref, wo_ref, wgu_ref, wd_ref, wpg_ref, wpp_ref, gains_ref, *rest):
    if len(rest) == 2:
        y_ref, rl_ref = rest
    else:
        ss_ref, ss0_ref, svn_ref, swin_ref, y_ref, swin_out_ref, so_ref, rl_ref = rest
        _window_values(ss_ref, ss0_ref, svn_ref, swin_ref, swin_out_ref, so_ref)
    ffn = wd_ref.shape[0]
    tm = x_ref.shape[0]
    sub = min(tm, POST_SUB)
    subs = [slice(lo, lo + sub) for lo in range(0, tm, sub)]
    ns = len(subs)
    n_m = tm // 16
    for p in range(AT_PAIRS):
        for c in range(16):
            piece = oa_ref[p, :, c * LANES:(c + 1) * LANES].astype(F32)
            for blk in range(n_m // 8):
                rl_ref[p, pl.ds(33 * 8 * blk + 2 * c, 8, stride=33), :] = piece[8 * blk:8 * blk + 8]

    def attn_rows(r):
        return [jnp.concatenate([rl_ref[p, pl.ds(33 * m + 16 * half, 8, stride=2), :]
                                 for m in range(r.start // 16, r.stop // 16) for half in range(2)],
                                axis=0).astype(BF16) for p in range(AT_PAIRS)]

    mix = [_dot(jnp.concatenate([od_ref[r, :]] + attn_rows(r), axis=1), wo_ref[...]) for r in subs]
    x1 = [x_ref[r, :] + _rms(m, gains_ref[0:1, :]) for r, m in zip(subs, mix)]
    h2 = [_rms(t, gains_ref[1:2, :]).astype(BF16) for t in x1]

    work = [(lo, s) for s in range(ns) for lo in range(0, ffn, FFN_CHUNK)]
    gate_up = {}

    def issue(lo, s):
        gate_up[lo, s] = (_dot(h2[s], wgu_ref[:, lo:lo + FFN_CHUNK]),
                          _dot(h2[s], wgu_ref[:, ffn + lo:ffn + lo + FFN_CHUNK]))

    issue(*work[0])
    down = [None] * ns
    for idx, (lo, s) in enumerate(work):
        if idx + 1 < len(work):
            issue(*work[idx + 1])
        gt, up = gate_up.pop((lo, s))
        part = _dot((_silu(gt) * up).astype(BF16), wd_ref[lo:lo + FFN_CHUNK, :])
        down[s] = part if down[s] is None else down[s] + part

    x2 = [a + _rms(d, gains_ref[2:3, :]) for a, d in zip(x1, down)]
    gate = [_dot(t.astype(BF16), wpg_ref[...]) for t in x2]
    pe = [_dot(p_ref[r, :].astype(BF16), wpp_ref[...]) for r in subs]
    for r, t, g, e in zip(subs, x2, gate, pe):
        y_ref[r, :] = t + _rms(_sigmoid(g) * e, gains_ref[3:4, :])


def _post(x, o_dn, o_at, p, wo, wgu, wd, wpg, wpp, gains, window=None):
    B, S, D = x.shape
    tm = min(S, POST_TM) if window is None else B * S // window[3].shape[0]
    assert S % tm == 0 and tm % 128 == 0
    nt = S // tm
    tok = lambda width: pl.BlockSpec((None, tm, width), lambda b, i: (b, i, 0))
    in_specs = [tok(D), tok(DN_W),
                pl.BlockSpec((None, AT_PAIRS, tm // 16, 16 * LANES), lambda b, i: (b, 0, i, 0)),
                tok(p.shape[-1]),
                _const_spec(wo.shape), _const_spec(wgu.shape), _const_spec(wd.shape),
                _const_spec(wpg.shape), _const_spec(wpp.shape), _const_spec(gains.shape)]
    out_specs = [tok(D)]
    out_shape = [jax.ShapeDtypeStruct((B, S, D), F32)]
    args = [x, o_dn, o_at, p, wo, wgu, wd, wpg, wpp, gains]
    if window is not None:
        Bs, H, E, W = window[3].shape
        assert B * S == Bs * tm
        col, slab, score, score0 = _window_specs(nt)
        in_specs += [score, score0, col, slab]
        out_specs += [slab, col]
        out_shape += [jax.ShapeDtypeStruct((Bs, H, E, W), F32), jax.ShapeDtypeStruct((Bs, E, H), F32)]
        args += list(window)
    out = pl.pallas_call(
        _post_kernel,
        grid=(B, nt),
        in_specs=in_specs,
        out_specs=out_specs,
        out_shape=out_shape,
        scratch_shapes=[pltpu.VMEM((AT_PAIRS, -(-33 * (tm // 16) // 8) * 8, LANES), F32)],
        compiler_params=pltpu.CompilerParams(dimension_semantics=("arbitrary", "arbitrary"),
                                             vmem_limit_bytes=VMEM_LIMIT),
        name="post",
    )(*args)
    return out[0] if window is None else out


def _layer_weights(w_in, conv_w, a_log, dt_bias, g_dn_out, w_out, g_post_mix, g_pre_ffn, w_gate_up, w_down,
                   g_post_ffn, w_ple_proj, w_ple_gate, g_ple):
    c0 = CONV_CH
    c1 = c0 + DN_W
    c2 = c1 + 2 * DN_HEADS
    w_ba = jnp.pad(w_in[:, c1:c2], ((0, 0), (0, LANES - 2 * DN_HEADS)))
    w = jnp.concatenate([w_in[:, :c1], w_in[:, c2:], w_ba], axis=1).astype(BF16)
    assert w.shape[1] == IN_COLS_PAD
    ap = jnp.zeros((8, LANES), F32)
    ap = ap.at[0, DN_HEADS:2 * DN_HEADS].set(a_log.astype(F32))
    ap = ap.at[1, DN_HEADS:2 * DN_HEADS].set(dt_bias.astype(F32))
    gains = jnp.zeros((8, w_out.shape[1]), F32)
    for r, g in enumerate((g_post_mix, g_pre_ffn, g_post_ffn, g_ple)):
        gains = gains.at[r].set(g.astype(F32))
    return dict(w=w, cw=conv_w.astype(F32), ap=ap, gn=g_dn_out.astype(F32)[None], wo=w_out.astype(BF16),
                wgu=w_gate_up.astype(BF16), wd=w_down.astype(BF16), wpg=w_ple_gate.astype(BF16),
                wpp=w_ple_proj.astype(BF16), gains=gains)


def _layer(xp, xs, pp, ps, cache_conv, state, win_k, win_v, g_pre, lw, tables, stab, self_tab):
    B, S, D = xp.shape
    Bs, T, _ = xs.shape
    assert T == 1
    post_w = (lw["wo"], lw["wgu"], lw["wd"], lw["wpg"], lw["wpp"], lw["gains"])
    to_minor = lambda t: t.transpose(0, 2, 3, 1)
    from_minor = lambda t: t.transpose(0, 3, 1, 2)

    cols = lambda t: t.reshape(Bs, AT_HEADS, AT_HD).transpose(0, 2, 1)

    xs2 = xs.reshape(Bs, D)
    qt, kt, v_s, z_s, ba_s, qa_s, ka_s, va_s, nc_s = _inproj_sample(
        xs2, g_pre, lw["w"], lw["cw"], lw["ap"], cache_conv.transpose(1, 0, 2))
    o_dn_s, s_new = _delta_sample(qt, kt, v_s, z_s, ba_s, lw["gn"], state)

    (dn, ba, qa, ka, va, wk, wv, nc, new_wk, s_sc, s_sc0) = _inproj_prompt(
        xp, g_pre, lw["w"], lw["cw"], lw["ap"], cols(qa_s), cols(ka_s), to_minor(win_k), stab, self_tab)
    o_dn, s_fin = _delta_prompt(dn, ba, lw["gn"], tb=DELTA_TB)
    o_at = _attn_prompt(qa, ka, va, tables)
    yp, new_wv, o_at_s = _post(xp, o_dn, o_at, pp, *post_w, window=(s_sc, s_sc0, cols(va_s), to_minor(win_v)))

    o_at_s = (o_at_s.transpose(0, 2, 1).reshape(Bs // 16, 16, AT_PAIRS, LANES).transpose(2, 0, 1, 3)
              .reshape(1, AT_PAIRS, Bs // 16, 16 * LANES).astype(BF16))
    ys = _post(xs2[None], o_dn_s.reshape(1, Bs, DN_W).astype(BF16), o_at_s, ps.reshape(1, Bs, -1), *post_w)

    n_keep = wk.shape[2]
    return (yp, ys.reshape(Bs, 1, D),
            nc[:, 8 - (CONV_W - 1):], s_fin,
            from_minor(wk.reshape(B, AT_HEADS, AT_HD, n_keep)), from_minor(wv.reshape(B, AT_HEADS, AT_HD, n_keep)),
            nc_s.transpose(1, 0, 2), s_new, from_minor(new_wk), from_minor(new_wv))


def kernel(x_prompt, x_sample, cache_conv, state_delta, cache_win_k, cache_win_v, p_prompt, p_sample, rel_bias, g_pre_mix, w_in, conv_w, a_log, dt_bias, g_dn_out, w_out, g_post_mix, g_pre_ffn, w_gate_up, w_down, g_post_ffn, w_ple_proj, w_ple_gate, g_ple):
    depth = w_in.shape[0]
    tables = _band_tables(rel_bias)
    stab, self_tab = _sample_tables(rel_bias)
    yp, ys = x_prompt, x_sample
    outs = [[] for _ in range(8)]
    for i in range(depth):
        lw = _layer_weights(w_in[i], conv_w[i], a_log[i], dt_bias[i], g_dn_out[i], w_out[i], g_post_mix[i],
                            g_pre_ffn[i], w_gate_up[i], w_down[i], g_post_ffn[i], w_ple_proj[i], w_ple_gate[i],
                            g_ple[i])
        g_pre = g_pre_mix[i].astype(F32)[None]
        yp, ys, *rest = _layer(yp, ys, p_prompt[i], p_sample[i], cache_conv[i], state_delta[i], cache_win_k[i],
                               cache_win_v[i], g_pre, lw, tables, stab, self_tab)
        for lst, val in zip(outs, rest):
            lst.append(val)
    return (yp, ys) + tuple(jnp.stack(o) for o in outs)
```

```python
import functools
import math

import numpy as np
import jax
import jax.numpy as jnp
from jax import lax
from jax.experimental import pallas as pl
from jax.experimental.pallas import tpu as pltpu

DN_HEADS = 4
DN_D = 128
CONV_W = 4
AT_HEADS = 8
AT_HD = 64
AT_PAIRS = AT_HEADS // 2
DIL_PAIRS = ((128, 1), (512, 4), (2048, 16))
WIN_MAX = 2048
BAND = 128
NUM_BUCKETS = 32
MAX_DIST = 2048
RMS_EPS = 1e-6
NEG = -1e30
LOG2E = math.log2(math.e)

DN_W = DN_HEADS * DN_D
CONV_CH = 3 * DN_W
AT_W = AT_HEADS * AT_HD
LANES = 128
SUPER = 16 * BAND

V7X_VMEM_BYTES = 64 * 1024 * 1024
VMEM_LIMIT = V7X_VMEM_BYTES - 8 * 1024 * 1024

INPROJ_SUB = 256
DELTA_TB = 256
POST_TM = 512
POST_SUB = 128
FFN_CHUNK = 256

F32 = jnp.float32
BF16 = jnp.bfloat16


def _t5_bucket(n):
    max_exact = NUM_BUCKETS // 2
    n = np.asarray(n)
    large = max_exact + (np.log(np.maximum(n, 1) / max_exact) / math.log(MAX_DIST / max_exact)
                         * (NUM_BUCKETS - max_exact)).astype(np.int32)
    large = np.minimum(large, NUM_BUCKETS - 1)
    return np.where(n < max_exact, n, large).astype(np.int32)


def _sigmoid(x):
    return 1.0 / (1.0 + jnp.exp(-x))


def _silu(x):
    return x * _sigmoid(x)


def _softplus(x):
    return jnp.maximum(x, 0.0) + jnp.log1p(jnp.exp(-jnp.abs(x)))


def _rms(x, g):
    return x * lax.rsqrt(jnp.mean(x * x, axis=-1, keepdims=True) + RMS_EPS) * g


def _dot(a, b):
    return jnp.dot(a, b, preferred_element_type=F32)


def _dot_nt(a, b):
    return lax.dot_general(a, b, (((1,), (1,)), ((), ())), preferred_element_type=F32)


def _const_spec(shape):
    nd = len(shape)
    return pl.BlockSpec(shape, lambda *_: (0,) * nd, pipeline_mode=pl.Buffered(1))


C_CONV = 0
C_Z = CONV_CH
C_Q = C_Z + DN_W
C_K = C_Q + AT_W
C_V = C_K + AT_W
C_BA = C_V + AT_W
IN_COLS_PAD = C_BA + LANES


def _dn_activations(c, ba_raw, ap_ref):
    c = _silu(c)
    qs, ks = [], []
    for h in range(DN_HEADS):
        qh = c[:, h * DN_D:(h + 1) * DN_D]
        kh = c[:, DN_W + h * DN_D:DN_W + (h + 1) * DN_D]
        qs.append(qh * lax.rsqrt(jnp.sum(qh * qh, axis=-1, keepdims=True) + 1e-6) * (DN_D ** -0.5))
        ks.append(kh * lax.rsqrt(jnp.sum(kh * kh, axis=-1, keepdims=True) + 1e-6))
    q = jnp.concatenate(qs, axis=1)
    k = jnp.concatenate(ks, axis=1)
    v = c[:, 2 * DN_W:]
    lane = lax.broadcasted_iota(jnp.int32, ba_raw.shape, 1)
    beta = _sigmoid(ba_raw)
    gdec = -jnp.exp(ap_ref[0:1, :]) * _softplus(ba_raw + ap_ref[1:2, :])
    ba = jnp.where(lane < DN_HEADS, beta, gdec)
    return q, k, v, ba


def _inproj_prompt_kernel(x_ref, g_ref, w_ref, cw_ref, ap_ref, sq_ref, skn_ref, swin_ref, stab_ref, stab0_ref,
                          dn_ref, ba_ref, qa_ref, ka_ref, va_ref,
                          wk_ref, wv_ref, nc_ref, swin_out_ref, ss_ref, ss0_ref, buf_ref, rl_ref, *, tm):
    i = pl.program_id(1)

    @pl.when(i == 0)
    def _():
        buf_ref[0:8, :] = jnp.zeros((8, CONV_CH), F32)

    sub = min(tm, INPROJ_SUB)
    subs = [slice(lo, lo + sub) for lo in range(0, tm, sub)]
    hs = [_rms(x_ref[r, :], g_ref[...]).astype(BF16) for r in subs]
    us = [_dot(h, w_ref[:, C_CONV:C_Z]) for h in hs]
    for r, u in zip(subs, us):
        buf_ref[8 + r.start:8 + r.stop, :] = u
    bas = [_dot(h, w_ref[:, C_BA:C_BA + LANES]) for h in hs]
    ctx = CONV_W - 1
    for r, h, u, ba_raw in zip(subs, hs, us, bas):
        c = u * cw_ref[ctx:CONV_W, :]
        for t in range(ctx):
            c = c + buf_ref[8 - ctx + t + r.start:8 - ctx + t + r.stop, :] * cw_ref[t:t + 1, :]
        q, k, v, ba = _dn_activations(c, ba_raw, ap_ref)
        dn_ref[r, 0:DN_W] = q.astype(BF16)
        dn_ref[r, DN_W:2 * DN_W] = k.astype(BF16)
        dn_ref[r, 2 * DN_W:3 * DN_W] = v.astype(BF16)
        ba_ref[r, :] = ba
        dn_ref[r, 3 * DN_W:4 * DN_W] = _dot(h, w_ref[:, C_Z:C_Q]).astype(BF16)

        qa = _dot(h, w_ref[:, C_Q:C_K]) * (AT_HD ** -0.5 * LOG2E)
        ka = _dot(h, w_ref[:, C_K:C_V])
        va = _dot(h, w_ref[:, C_V:C_BA])

        wk_ref[:, r] = ka.T
        wv_ref[:, r] = va.T

        for t_idx, (val, dst_ref) in enumerate(((qa, qa_ref), (ka, ka_ref), (va, va_ref))):
            for p in range(AT_PAIRS):
                slab = t_idx * AT_PAIRS + p
                for g in range(sub // 16):
                    m = (r.start + 16 * g) // 16
                    for half in range(2):
                        rows = slice(16 * g + 8 * half, 16 * g + 8 * half + 8)
                        rl_ref[slab, pl.ds(33 * m + 16 * half, 8, stride=2), :] = val[rows, p * LANES:(p + 1) * LANES]
    n_m = tm // 16
    for t_idx, dst_ref in enumerate((qa_ref, ka_ref, va_ref)):
        for p in range(AT_PAIRS):
            slab = t_idx * AT_PAIRS + p
            for c in range(16):
                piece = jnp.concatenate(
                    [rl_ref[slab, pl.ds(33 * 8 * blk + 2 * c, 8, stride=33), :] for blk in range(n_m // 8)], axis=0)
                dst_ref[p, :, c * LANES:(c + 1) * LANES] = piece.astype(BF16)
    last = us[-1][sub - 8:sub, :]
    buf_ref[0:8, :] = last
    nc_ref[...] = last

    _window_keys(sq_ref, skn_ref, swin_ref, stab_ref, stab0_ref, swin_out_ref, ss_ref, ss0_ref)


def _inproj_prompt(x, g, w, cw, ap, s_q, s_kn, s_win_k, s_tab, s_tab0):
    B, S, D = x.shape
    Bs, H, E, W = s_win_k.shape
    n_keep = min(WIN_MAX, S)
    tm = B * S // Bs
    assert B * S == Bs * tm and S % tm == 0 and n_keep % tm == 0 and tm % 128 == 0
    nt = S // tm
    first_keep = (S - n_keep) // tm
    col, slab, score, score0 = _window_specs(nt)
    tok = lambda width: pl.BlockSpec((None, tm, width), lambda b, i: (b, i, 0))
    pair = pl.BlockSpec((None, AT_PAIRS, tm // 16, 16 * LANES), lambda b, i: (b, 0, i, 0))
    keep = pl.BlockSpec((None, AT_W, tm), lambda b, i: (b, 0, jnp.maximum(i - first_keep, 0)))
    bf = lambda width: jax.ShapeDtypeStruct((B, S, width), BF16)
    pairs = jax.ShapeDtypeStruct((B, AT_PAIRS, S // 16, 16 * LANES), BF16)
    relayout_rows = -(-33 * (tm // 16) // 8) * 8
    return pl.pallas_call(
        functools.partial(_inproj_prompt_kernel, tm=tm),
        grid=(B, nt),
        in_specs=[tok(D), _const_spec((1, D)), _const_spec(w.shape), _const_spec(cw.shape),
                  _const_spec(ap.shape), col, col, slab, _const_spec(s_tab.shape), _const_spec(s_tab0.shape)],
        out_specs=[tok(4 * DN_W), tok(LANES), pair, pair, pair,
                   keep, keep, pl.BlockSpec((None, 8, CONV_CH), lambda b, i: (b, 0, 0)),
                   slab, score, score0],
        out_shape=[bf(4 * DN_W), jax.ShapeDtypeStruct((B, S, LANES), F32),
                   pairs, pairs, pairs,
                   jax.ShapeDtypeStruct((B, AT_W, n_keep), F32), jax.ShapeDtypeStruct((B, AT_W, n_keep), F32),
                   jax.ShapeDtypeStruct((B, 8, CONV_CH), F32),
                   jax.ShapeDtypeStruct(s_win_k.shape, F32), jax.ShapeDtypeStruct((Bs, H, W), F32),
                   jax.ShapeDtypeStruct((Bs, H, LANES), F32)],
        scratch_shapes=[pltpu.VMEM((tm + 8, CONV_CH), F32),
                        pltpu.VMEM((3 * AT_PAIRS, relayout_rows, LANES), F32)],
        compiler_params=pltpu.CompilerParams(dimension_semantics=("arbitrary", "arbitrary"),
                                             vmem_limit_bytes=VMEM_LIMIT),
        name="inproj_prompt",
    )(x, g, w, cw, ap, s_q, s_kn, s_win_k, s_tab, s_tab0)


def _inproj_sample_kernel(x_ref, g_ref, w_ref, cw_ref, ap_ref, cc_ref,
                          qt_ref, kt_ref, v_ref, z_ref, ba_ref, qa_ref, ka_ref, va_ref, nc_ref):
    x = x_ref[...]
    h = _rms(x, g_ref[...]).astype(BF16)
    u = _dot(h, w_ref[:, C_CONV:C_Z])
    c = u * cw_ref[CONV_W - 1:CONV_W, :]
    for t in range(CONV_W - 1):
        c = c + cc_ref[t] * cw_ref[t:t + 1, :]
    for t in range(1, CONV_W - 1):
        nc_ref[t - 1] = cc_ref[t]
    nc_ref[CONV_W - 2] = u

    ba_raw = _dot(h, w_ref[:, C_BA:C_BA + LANES])
    q, k, v, ba = _dn_activations(c, ba_raw, ap_ref)
    qt_ref[...] = q.T
    kt_ref[...] = k.T
    v_ref[...] = v
    ba_ref[...] = ba
    z_ref[...] = _dot(h, w_ref[:, C_Z:C_Q])
    qa_ref[...] = _dot(h, w_ref[:, C_Q:C_K]) * (AT_HD ** -0.5)
    ka_ref[...] = _dot(h, w_ref[:, C_K:C_V])
    va_ref[...] = _dot(h, w_ref[:, C_V:C_BA])


def _inproj_sample(x, g, w, cw, ap, conv_ctx):
    B, D = x.shape
    f = lambda *s: jax.ShapeDtypeStruct(s, F32)
    return pl.pallas_call(
        _inproj_sample_kernel,
        out_shape=[f(DN_W, B), f(DN_W, B), f(B, DN_W), f(B, DN_W), f(B, LANES),
                   f(B, AT_W), f(B, AT_W), f(B, AT_W), f(CONV_W - 1, B, CONV_CH)],
        compiler_params=pltpu.CompilerParams(vmem_limit_bytes=VMEM_LIMIT),
        name="inproj_sample",
    )(x, g, w, cw, ap, conv_ctx)


DN_CHUNK = 64


def _delta_prompt_kernel(dn_ref, ba_ref, gn_ref, o_ref, s_out_ref, s_ref, *, nb, tb):
    i = pl.program_id(0)
    K0, V0, Z0 = DN_W, 2 * DN_W, 3 * DN_W
    C = DN_CHUNK
    nc = tb // C

    @pl.when(i == 0)
    def _():
        s_ref[...] = jnp.zeros(s_ref.shape, F32)

    rt = lax.broadcasted_iota(jnp.int32, (tb, tb), 0)
    ct = lax.broadcasted_iota(jnp.int32, (tb, tb), 1)
    tri = jnp.where((rt >= ct) & ((rt // C) == (ct // C)), 1.0, 0.0).astype(F32)
    ri = lax.broadcasted_iota(jnp.int32, (C, C), 0)
    ci = lax.broadcasted_iota(jnp.int32, (C, C), 1)
    eye = jnp.where(ri == ci, 1.0, 0.0).astype(F32)
    gn = gn_ref[...]

    units = [(b, h, c) for b in range(nb) for h in range(DN_HEADS) for c in range(nc)]
    pre = {}
    for b in range(nb):
        ba = ba_ref[b]
        gc_all = jnp.dot(tri, ba, preferred_element_type=F32, precision=lax.Precision.HIGHEST)
        pre[b] = (ba, gc_all, gc_all.T, jnp.exp(gc_all))
    kts = {(b, h): dn_ref[b, :, K0 + h * DN_D:K0 + (h + 1) * DN_D].astype(F32).T
           for b in range(nb) for h in range(DN_HEADS)}

    st = {}
    for (b, h, c) in units:
        ba, gc_all, gc_t, eg_all = pre[b]
        rows = slice(c * C, (c + 1) * C)
        lanes = slice(h * DN_D, (h + 1) * DN_D)
        gl = DN_HEADS + h
        kf = dn_ref[b, rows, K0 + h * DN_D:K0 + (h + 1) * DN_D].astype(F32)
        qf = dn_ref[b, rows, lanes].astype(F32)
        vf = dn_ref[b, rows, V0 + h * DN_D:V0 + (h + 1) * DN_D].astype(F32)
        beta = ba[rows, h:h + 1]
        gcol = gc_all[rows, gl:gl + 1]
        grow = gc_t[gl:gl + 1, rows]
        egc = eg_all[rows, gl:gl + 1]
        glast = gc_all[c * C + C - 1:c * C + C, gl:gl + 1]
        kb = kf * beta
        st[b, h, c] = dict(
            dec=jnp.exp(jnp.where(ri >= ci, gcol - grow, NEG)),
            lhs=jnp.concatenate([kb, qf], axis=0).astype(BF16),
            kbf=kf.astype(BF16),
            rhs=jnp.concatenate([vf * beta, kb * egc], axis=1).astype(BF16),
            qdec=(qf * egc).astype(BF16),
            kdt=(kts[b, h][:, rows] * jnp.exp(glast - grow)).astype(BF16),
            glast=jnp.exp(glast))
    for u in units:
        st[u]["both"] = _dot_nt(st[u]["lhs"], st[u]["kbf"])
    for u in units:
        d = st[u]
        d["a"] = jnp.where(ri > ci, d["both"][:C] * d["dec"], 0.0)
        d["intra"] = (d["both"][C:] * d["dec"]).astype(BF16)

    pairs = [(b, h, c) for b in range(nb) for h in range(0, DN_HEADS, 2) for c in range(nc)]
    left = lax.broadcasted_iota(jnp.int32, (1, 2 * C), 1) < C
    eye2 = jnp.concatenate([eye, eye], axis=1)

    def block_diag(m):
        mb = m.astype(BF16)
        zero = jnp.zeros_like(mb)
        return jnp.concatenate([jnp.where(left, mb, zero), jnp.where(left, zero, mb)], axis=0)

    pst = {}
    for (b, h, c) in pairs:
        a2 = jnp.concatenate([st[b, h, c]["a"], st[b, h + 1, c]["a"]], axis=1)
        pst[b, h, c] = dict(xm=eye2 - a2, pm=-a2)
    for pr in pairs:
        d = pst[pr]
        d["pm"] = _dot(d["pm"].astype(BF16), block_diag(d["pm"]))
    for _ in range(4):
        for pr in pairs:
            d = pst[pr]
            d["r"] = _dot(jnp.concatenate([d["xm"], d["pm"]], axis=0).astype(BF16), block_diag(d["pm"]))
        for pr in pairs:
            d = pst[pr]
            d["xm"] = d["xm"] + d["r"][:C]
            d["pm"] = d["r"][C:]
    for pr in pairs:
        d = pst[pr]
        d["r"] = _dot(d["xm"].astype(BF16), block_diag(d["pm"]))
    for (b, h, c) in pairs:
        d = pst[b, h, c]
        inv = (d["xm"] + d["r"]).astype(BF16)
        st[b, h, c]["inv"] = inv[:, :C]
        st[b, h + 1, c]["inv"] = inv[:, C:]
    for u in units:
        d = st[u]
        d["uw"] = _dot(d["inv"], d["rhs"])

    chains = [(b, h) for b in range(nb) for h in range(DN_HEADS)]
    state = {bh: s_ref[bh[0], bh[1]] for bh in chains}
    for c in range(nc):
        rows = slice(c * C, (c + 1) * C)
        r2 = {}
        for (b, h) in chains:
            d = st[b, h, c]
            lhs = jnp.concatenate([d["uw"][:, DN_D:].astype(BF16), d["qdec"]], axis=0)
            r2[b, h] = _dot(lhs, state[b, h].astype(BF16))
        vb = {bh: (st[bh[0], bh[1], c]["uw"][:, :DN_D] - r2[bh][:C]).astype(BF16) for bh in chains}
        for (b, h) in chains:
            d = st[b, h, c]
            o = r2[b, h][C:] + _dot(d["intra"], vb[b, h])
            state[b, h] = state[b, h] * d["glast"] + _dot(d["kdt"], vb[b, h])
            lanes = slice(h * DN_D, (h + 1) * DN_D)
            zf = dn_ref[b, rows, Z0 + h * DN_D:Z0 + (h + 1) * DN_D].astype(F32)
            o_ref[b, rows, lanes] = (_rms(o, gn) * _silu(zf)).astype(BF16)
    for (b, h) in chains:
        s_ref[b, h] = state[b, h]

    @pl.when(i == pl.num_programs(0) - 1)
    def _():
        s_out_ref[...] = s_ref[...]


def _delta_prompt(dn, ba, gn, *, tb):
    B, S, _ = dn.shape
    assert S % tb == 0 and tb % DN_CHUNK == 0
    tok = lambda width: pl.BlockSpec((B, tb, width), lambda i: (0, i, 0))
    st = pl.BlockSpec((B, DN_HEADS, DN_D, DN_D), lambda i: (0, 0, 0, 0))
    return pl.pallas_call(
        functools.partial(_delta_prompt_kernel, nb=B, tb=tb),
        grid=(S // tb,),
        in_specs=[tok(4 * DN_W), tok(LANES), _const_spec((1, DN_D))],
        out_specs=[tok(DN_W), st],
        out_shape=[jax.ShapeDtypeStruct((B, S, DN_W), BF16),
                   jax.ShapeDtypeStruct((B, DN_HEADS, DN_D, DN_D), F32)],
        scratch_shapes=[pltpu.VMEM((B, DN_HEADS, DN_D, DN_D), F32)],
        compiler_params=pltpu.CompilerParams(dimension_semantics=("arbitrary",),
                                             vmem_limit_bytes=VMEM_LIMIT),
        name="delta_prompt",
    )(dn, ba, gn)


SAMPLE_SEQS_PER_STEP = 8


def _delta_sample_kernel(qt_ref, kt_ref, v_ref, z_ref, ba_ref, gn_ref, s_ref, o_ref, s_out_ref):
    step = pl.program_id(0)
    nb = qt_ref.shape[1]
    per = v_ref.shape[0]
    lane = lax.broadcasted_iota(jnp.int32, (DN_D, nb), 1)
    gn = gn_ref[...]
    units = [(j, h) for j in range(per) for h in range(DN_HEADS)]
    rows = {h: slice(h * DN_D, (h + 1) * DN_D) for h in range(DN_HEADS)}
    sel = {j: lane == step * per + j for j in range(per)}
    qcol = {(j, h): jnp.sum(jnp.where(sel[j], qt_ref[rows[h], :], 0.0), axis=1, keepdims=True)
            for (j, h) in units}
    kcol = {(j, h): jnp.sum(jnp.where(sel[j], kt_ref[rows[h], :], 0.0), axis=1, keepdims=True)
            for (j, h) in units}
    egd = jnp.exp(ba_ref[...])
    s = {(j, h): s_ref[j, h] * egd[j:j + 1, DN_HEADS + h:DN_HEADS + h + 1] for (j, h) in units}
    kv = {u: jnp.sum(kcol[u] * s[u], axis=0, keepdims=True) for u in units}
    for (j, h) in units:
        delta = (v_ref[j:j + 1, rows[h]] - kv[j, h]) * ba_ref[j:j + 1, h:h + 1]
        s[j, h] = s[j, h] + kcol[j, h] * delta
        s_out_ref[j, h] = s[j, h]
    o = {u: jnp.sum(qcol[u] * s[u], axis=0, keepdims=True) for u in units}
    for (j, h) in units:
        o_ref[j:j + 1, rows[h]] = _rms(o[j, h], gn) * _silu(z_ref[j:j + 1, rows[h]])


def _delta_sample(qt, kt, v, z, ba, gn, state):
    B = v.shape[0]
    per = SAMPLE_SEQS_PER_STEP
    assert B % per == 0
    row = lambda width: pl.BlockSpec((per, width), lambda b: (b, 0))
    st = pl.BlockSpec((per, DN_HEADS, DN_D, DN_D), lambda b: (b, 0, 0, 0))
    return pl.pallas_call(
        _delta_sample_kernel,
        grid=(B // per,),
        in_specs=[_const_spec(qt.shape), _const_spec(kt.shape), row(DN_W), row(DN_W), row(LANES),
                  _const_spec((1, DN_D)), st],
        out_specs=[row(DN_W), st],
        out_shape=[jax.ShapeDtypeStruct((B, DN_W), F32), jax.ShapeDtypeStruct(state.shape, F32)],
        compiler_params=pltpu.CompilerParams(dimension_semantics=("arbitrary",)),
        name="delta_sample",
    )(qt, kt, v, z, ba, gn, state)


def _stack_position(d):
    a = np.arange(BAND)
    rows = 8 * d
    return (16 // d) * (a % rows) + a // rows


def _bias_lookup(rel_bias, bucket):
    onehot = jnp.asarray(bucket[..., None] == np.arange(NUM_BUCKETS), dtype=BF16).astype(F32)
    return jnp.einsum("...n,nh->h...", onehot, rel_bias.astype(F32), precision=lax.Precision.HIGHEST)


def _band_tables(rel_bias):
    tables = []
    for w, d in DIL_PAIRS:
        assert w // d == BAND
        pos = _stack_position(d)
        dist = pos[:, None] - np.concatenate([pos - BAND, pos])[None, :]
        valid = (dist >= 0) & (dist <= BAND)
        bucket = _t5_bucket(np.clip(dist, 0, BAND) * d)
        bias = jnp.where(valid[None], _bias_lookup(rel_bias, bucket) * LOG2E, NEG)
        tables.append(bias.reshape(AT_PAIRS, 2 * BAND, 2 * BAND))
    return jnp.stack(tables)


def _attn_unit(q, kk, vv, table, pen):
    lane = lax.broadcasted_iota(jnp.int32, (1, LANES), 1)
    first = lane < AT_HD
    zero = jnp.zeros_like(q)
    q2 = jnp.concatenate([jnp.where(first, q, zero), jnp.where(first, zero, q)], axis=0)
    s = _dot_nt(q2, kk) + table
    if pen is not None:
        s = s + pen
    m = jnp.max(s, axis=1, keepdims=True)
    p = jnp.exp2(s - m)
    l = jnp.sum(p, axis=1, keepdims=True)
    pv = _dot(p.astype(BF16), vv)
    acc = jnp.where(first, pv[:BAND], pv[BAND:])
    mt = jnp.where(first, m[:BAND], m[BAND:])
    lt = jnp.where(first, l[:BAND], l[BAND:])
    return acc, mt, lt


def _attn_prompt_kernel(q_ref, k_ref, v_ref, t_ref, o_ref,
                        k16, v16, q4, k4, v4, q1, k1, v1, acc_ref, m_ref, l_ref):
    n = pl.program_id(2)

    @pl.when(n == 0)
    def _():
        zb = jnp.zeros((BAND, LANES), BF16)
        for rho in range(16):
            k16[rho, 0:BAND, :] = zb
            v16[rho, 0:BAND, :] = zb
        for rho in range(4):
            k4[rho, 0:BAND, :] = zb
            v4[rho, 0:BAND, :] = zb
        k1[0:BAND, :] = zb
        v1[0:BAND, :] = zb

    @pl.when(n > 0)
    def _():
        for rho in range(16):
            k16[rho, 0:BAND, :] = k16[rho, BAND:2 * BAND, :]
            v16[rho, 0:BAND, :] = v16[rho, BAND:2 * BAND, :]
        for rho in range(4):
            k4[rho, 0:BAND, :] = k4[rho, 4 * BAND:5 * BAND, :]
            v4[rho, 0:BAND, :] = v4[rho, 4 * BAND:5 * BAND, :]
        k1[0:BAND, :] = k1[SUPER:SUPER + BAND, :]
        v1[0:BAND, :] = v1[SUPER:SUPER + BAND, :]

    for rho in range(16):
        cols = slice(rho * LANES, (rho + 1) * LANES)
        k16[rho, BAND:2 * BAND, :] = k_ref[:, cols]
        v16[rho, BAND:2 * BAND, :] = v_ref[:, cols]
    for rho in range(4):
        for blk in range(4):
            for j in range(4):
                src = (slice(32 * blk, 32 * blk + 32), slice((4 * j + rho) * LANES, (4 * j + rho + 1) * LANES))
                dst = slice(BAND * blk + 32 * j, BAND * blk + 32 * j + 32)
                q4[rho, dst, :] = q_ref[src]
                k4[rho, BAND + dst.start:BAND + dst.stop, :] = k_ref[src]
                v4[rho, BAND + dst.start:BAND + dst.stop, :] = v_ref[src]
    for src_ref, dst_ref, off in ((q_ref, q1, 0), (k_ref, k1, BAND), (v_ref, v1, BAND)):
        for gam in range(8):
            lo, hi = [], []
            for r in range(16):
                piece = src_ref[16 * gam:16 * gam + 16, r * LANES:(r + 1) * LANES].astype(F32)
                lo.append(piece[0:8])
                hi.append(piece[8:16])
            dst_ref[off + BAND * (2 * gam):off + BAND * (2 * gam + 1), :] = jnp.concatenate(lo, axis=0).astype(BF16)
            dst_ref[off + BAND * (2 * gam + 1):off + BAND * (2 * gam + 2), :] = jnp.concatenate(hi, axis=0).astype(BF16)

    lane2 = lax.broadcasted_iota(jnp.int32, (1, 2 * BAND), 1)
    pen = jnp.where((lane2 < BAND) & (n == 0), NEG, 0.0).astype(F32)

    t16 = t_ref[2]
    for rho in range(16):
        cols = slice(rho * LANES, (rho + 1) * LANES)
        acc, mt, lt = _attn_unit(q_ref[:, cols], k16[rho], v16[rho], t16, pen)
        acc_ref[2, :, cols] = acc
        m_ref[2, :, cols] = mt
        l_ref[2, :, cols] = lt

    def park(g, rows, cols, acc, mt, lt):
        acc_ref[g, rows, cols] = acc
        m_ref[g, rows, cols] = mt
        l_ref[g, rows, cols] = lt

    t4 = t_ref[1]
    for rho in range(4):
        for blk in range(4):
            acc, mt, lt = _attn_unit(q4[rho, BAND * blk:BAND * (blk + 1), :],
                                     k4[rho, BAND * blk:BAND * (blk + 2), :],
                                     v4[rho, BAND * blk:BAND * (blk + 2), :],
                                     t4, pen if blk == 0 else None)
            for j in range(4):
                sub = slice(32 * j, 32 * j + 32)
                park(1, slice(32 * blk, 32 * blk + 32), slice((4 * j + rho) * LANES, (4 * j + rho + 1) * LANES),
                     acc[sub], mt[sub], lt[sub])

    t1 = t_ref[0]
    for blk in range(16):
        acc, mt, lt = _attn_unit(q1[BAND * blk:BAND * (blk + 1), :],
                                 k1[BAND * blk:BAND * (blk + 2), :],
                                 v1[BAND * blk:BAND * (blk + 2), :],
                                 t1, pen if blk == 0 else None)
        for r in range(16):
            sub = slice(8 * r, 8 * r + 8)
            park(0, slice(8 * blk, 8 * blk + 8), slice(r * LANES, (r + 1) * LANES),
                 acc[sub], mt[sub], lt[sub])

    for c in range(16):
        cols = slice(c * LANES, (c + 1) * LANES)
        ms = [m_ref[g, :, cols] for g in range(3)]
        m_all = jnp.maximum(jnp.maximum(ms[0], ms[1]), ms[2])
        w = [jnp.exp2(m - m_all) for m in ms]
        num = sum(acc_ref[g, :, cols] * w[g] for g in range(3))
        den = sum(l_ref[g, :, cols] * w[g] for g in range(3))
        o_ref[:, cols] = (num / den).astype(BF16)


def _attn_prompt(q, k, v, tables):
    B, P, S16, _ = q.shape
    S = 16 * S16
    assert S % SUPER == 0
    nt = S // SUPER
    tile = pl.BlockSpec((None, None, BAND, 16 * LANES), lambda b, p, n: (b, p, n, 0))
    tab = pl.BlockSpec((3, None, 2 * BAND, 2 * BAND), lambda b, p, n: (0, p, 0, 0))
    vm = lambda *s: pltpu.VMEM(s, BF16)
    out = pl.pallas_call(
        _attn_prompt_kernel,
        grid=(B, P, nt),
        in_specs=[tile, tile, tile, tab],
        out_specs=tile,
        out_shape=jax.ShapeDtypeStruct((B, P, S // 16, 16 * LANES), BF16),
        scratch_shapes=[vm(16, 2 * BAND, LANES), vm(16, 2 * BAND, LANES),
                        vm(4, 4 * BAND, LANES), vm(4, 5 * BAND, LANES), vm(4, 5 * BAND, LANES),
                        vm(SUPER, LANES), vm(SUPER + BAND, LANES), vm(SUPER + BAND, LANES),
                        pltpu.VMEM((3, BAND, 16 * LANES), F32), pltpu.VMEM((3, BAND, 16 * LANES), F32),
                        pltpu.VMEM((3, BAND, 16 * LANES), F32)],
        compiler_params=pltpu.CompilerParams(dimension_semantics=("arbitrary", "arbitrary", "arbitrary"),
                                             vmem_limit_bytes=VMEM_LIMIT),
        name="attn_prompt",
    )(q, k, v, tables)
    return out


def _sample_tables(rel_bias):
    W = WIN_MAX
    j = W - np.arange(W)
    count = sum(((j % d == 0) & (j <= w)).astype(np.int64) for w, d in DIL_PAIRS)
    bias = _bias_lookup(rel_bias, _t5_bucket(j))
    tab = jnp.where(count[None] > 0, bias + np.log(np.maximum(count, 1))[None].astype(np.float32), NEG)
    self_tab = rel_bias[0].astype(F32) + math.log(len(DIL_PAIRS))
    return tab, jnp.broadcast_to(self_tab[:, None], (AT_HEADS, LANES))


def _window_keys(q_ref, kn_ref, k_ref, t_ref, t0_ref, ok_ref, s_ref, s0_ref):
    H, E, W = k_ref.shape
    last = lax.broadcasted_iota(jnp.int32, (1, W), 1) == W - 1
    for h in range(H):
        qc = q_ref[:, h:h + 1]
        knc = kn_ref[:, h:h + 1]
        kh = k_ref[h]
        s_ref[h:h + 1, :] = jnp.sum(kh * qc, axis=0, keepdims=True) + t_ref[h:h + 1, :]
        s0_ref[h:h + 1, :] = jnp.sum(qc * knc, axis=0, keepdims=True) + t0_ref[h:h + 1, :]
        ok_ref[h] = jnp.where(last, knc, pltpu.roll(kh, W - 1, axis=1))


def _window_values(s_ref, s0_ref, vn_ref, v_ref, ov_ref, o_ref):
    H, E, W = v_ref.shape
    last = lax.broadcasted_iota(jnp.int32, (1, W), 1) == W - 1
    s = s_ref[...]
    s0 = s0_ref[:, 0:1]
    m = jnp.maximum(jnp.max(s, axis=1, keepdims=True), s0)
    p = jnp.exp(s - m)
    p0 = jnp.exp(s0 - m)
    l = jnp.sum(p, axis=1, keepdims=True) + p0
    for h in range(H):
        vh = v_ref[h]
        vnc = vn_ref[:, h:h + 1]
        acc = jnp.sum(vh * p[h:h + 1, :], axis=1, keepdims=True) + p0[h:h + 1, :] * vnc
        o_ref[:, h:h + 1] = acc / l[h:h + 1, :]
        ov_ref[h] = jnp.where(last, vnc, pltpu.roll(vh, W - 1, axis=1))


def _window_specs(nt):
    seq = lambda b, i: b * nt + i
    col = pl.BlockSpec((None, AT_HD, AT_HEADS), lambda b, i: (seq(b, i), 0, 0))
    slab = pl.BlockSpec((None, AT_HEADS, AT_HD, WIN_MAX), lambda b, i: (seq(b, i), 0, 0, 0))
    score = pl.BlockSpec((None, AT_HEADS, WIN_MAX), lambda b, i: (seq(b, i), 0, 0))
    score0 = pl.BlockSpec((None, AT_HEADS, LANES), lambda b, i: (seq(b, i), 0, 0))
    return col, slab, score, score0


def _post_kernel(x_ref, od_ref, oa_ref, p_ref, wo_ref, wgu_ref, wd_ref, wpg_ref, wpp_ref, gains_ref, *rest):
    if len(rest) == 2:
        y_ref, rl_ref = rest
    else:
        ss_ref, ss0_ref, svn_ref, swin_ref, y_ref, swin_out_ref, so_ref, rl_ref = rest
        _window_values(ss_ref, ss0_ref, svn_ref, swin_ref, swin_out_ref, so_ref)
    ffn = wd_ref.shape[0]
    tm = x_ref.shape[0]
    sub = min(tm, POST_SUB)
    subs = [slice(lo, lo + sub) for lo in range(0, tm, sub)]
    ns = len(subs)
    n_m = tm // 16
    for p in range(AT_PAIRS):
        for c in range(16):
            piece = oa_ref[p, :, c * LANES:(c + 1) * LANES].astype(F32)
            for blk in range(n_m // 8):
                rl_ref[p, pl.ds(33 * 8 * blk + 2 * c, 8, stride=33), :] = piece[8 * blk:8 * blk + 8]

    def attn_rows(r):
        return [jnp.concatenate([rl_ref[p, pl.ds(33 * m + 16 * half, 8, stride=2), :]
                                 for m in range(r.start // 16, r.stop // 16) for half in range(2)],
                                axis=0).astype(BF16) for p in range(AT_PAIRS)]

    mix = [_dot(jnp.concatenate([od_ref[r, :]] + attn_rows(r), axis=1), wo_ref[...]) for r in subs]
    x1 = [x_ref[r, :] + _rms(m, gains_ref[0:1, :]) for r, m in zip(subs, mix)]
    h2 = [_rms(t, gains_ref[1:2, :]).astype(BF16) for t in x1]

    work = [(lo, s) for s in range(ns) for lo in range(0, ffn, FFN_CHUNK)]
    gate_up = {}

    def issue(lo, s):
        gate_up[lo, s] = (_dot(h2[s], wgu_ref[:, lo:lo + FFN_CHUNK]),
                          _dot(h2[s], wgu_ref[:, ffn + lo:ffn + lo + FFN_CHUNK]))

    issue(*work[0])
    down = [None] * ns
    for idx, (lo, s) in enumerate(work):
        if idx + 1 < len(work):
            issue(*work[idx + 1])
        gt, up = gate_up.pop((lo, s))
        part = _dot((_silu(gt) * up).astype(BF16), wd_ref[lo:lo + FFN_CHUNK, :])
        down[s] = part if down[s] is None else down[s] + part

    x2 = [a + _rms(d, gains_ref[2:3, :]) for a, d in zip(x1, down)]
    gate = [_dot(t.astype(BF16), wpg_ref[...]) for t in x2]
    pe = [_dot(p_ref[r, :].astype(BF16), wpp_ref[...]) for r in subs]
    for r, t, g, e in zip(subs, x2, gate, pe):
        y_ref[r, :] = t + _rms(_sigmoid(g) * e, gains_ref[3:4, :])


def _post(x, o_dn, o_at, p, wo, wgu, wd, wpg, wpp, gains, window=None):
    B, S, D = x.shape
    tm = min(S, POST_TM) if window is None else B * S // window[3].shape[0]
    assert S % tm == 0 and tm % 128 == 0
    nt = S // tm
    tok = lambda width: pl.BlockSpec((None, tm, width), lambda b, i: (b, i, 0))
    in_specs = [tok(D), tok(DN_W),
                pl.BlockSpec((None, AT_PAIRS, tm // 16, 16 * LANES), lambda b, i: (b, 0, i, 0)),
                tok(p.shape[-1]),
                _const_spec(wo.shape), _const_spec(wgu.shape), _const_spec(wd.shape),
                _const_spec(wpg.shape), _const_spec(wpp.shape), _const_spec(gains.shape)]
    out_specs = [tok(D)]
    out_shape = [jax.ShapeDtypeStruct((B, S, D), F32)]
    args = [x, o_dn, o_at, p, wo, wgu, wd, wpg, wpp, gains]
    if window is not None:
        Bs, H, E, W = window[3].shape
        assert B * S == Bs * tm
        col, slab, score, score0 = _window_specs(nt)
        in_specs += [score, score0, col, slab]
        out_specs += [slab, col]
        out_shape += [jax.ShapeDtypeStruct((Bs, H, E, W), F32), jax.ShapeDtypeStruct((Bs, E, H), F32)]
        args += list(window)
    out = pl.pallas_call(
        _post_kernel,
        grid=(B, nt),
        in_specs=in_specs,
        out_specs=out_specs,
        out_shape=out_shape,
        scratch_shapes=[pltpu.VMEM((AT_PAIRS, -(-33 * (tm // 16) // 8) * 8, LANES), F32)],
        compiler_params=pltpu.CompilerParams(dimension_semantics=("arbitrary", "arbitrary"),
                                             vmem_limit_bytes=VMEM_LIMIT),
        name="post",
    )(*args)
    return out[0] if window is None else out


def _layer_weights(w_in, conv_w, a_log, dt_bias, g_dn_out, w_out, g_post_mix, g_pre_ffn, w_gate_up, w_down,
                   g_post_ffn, w_ple_proj, w_ple_gate, g_ple):
    c0 = CONV_CH
    c1 = c0 + DN_W
    c2 = c1 + 2 * DN_HEADS
    w_ba = jnp.pad(w_in[:, c1:c2], ((0, 0), (0, LANES - 2 * DN_HEADS)))
    w = jnp.concatenate([w_in[:, :c1], w_in[:, c2:], w_ba], axis=1).astype(BF16)
    assert w.shape[1] == IN_COLS_PAD
    ap = jnp.zeros((8, LANES), F32)
    ap = ap.at[0, DN_HEADS:2 * DN_HEADS].set(a_log.astype(F32))
    ap = ap.at[1, DN_HEADS:2 * DN_HEADS].set(dt_bias.astype(F32))
    gains = jnp.zeros((8, w_out.shape[1]), F32)
    for r, g in enumerate((g_post_mix, g_pre_ffn, g_post_ffn, g_ple)):
        gains = gains.at[r].set(g.astype(F32))
    return dict(w=w, cw=conv_w.astype(F32), ap=ap, gn=g_dn_out.astype(F32)[None], wo=w_out.astype(BF16),
                wgu=w_gate_up.astype(BF16), wd=w_down.astype(BF16), wpg=w_ple_gate.astype(BF16),
                wpp=w_ple_proj.astype(BF16), gains=gains)


def _layer(xp, xs, pp, ps, cache_conv, state, win_k, win_v, g_pre, lw, tables, stab, self_tab):
    B, S, D = xp.shape
    Bs, T, _ = xs.shape
    assert T == 1
    post_w = (lw["wo"], lw["wgu"], lw["wd"], lw["wpg"], lw["wpp"], lw["gains"])
    to_minor = lambda t: t.transpose(0, 2, 3, 1)
    from_minor = lambda t: t.transpose(0, 3, 1, 2)

    cols = lambda t: t.reshape(Bs, AT_HEADS, AT_HD).transpose(0, 2, 1)

    xs2 = xs.reshape(Bs, D)
    qt, kt, v_s, z_s, ba_s, qa_s, ka_s, va_s, nc_s = _inproj_sample(
        xs2, g_pre, lw["w"], lw["cw"], lw["ap"], cache_conv.transpose(1, 0, 2))
    o_dn_s, s_new = _delta_sample(qt, kt, v_s, z_s, ba_s, lw["gn"], state)

    (dn, ba, qa, ka, va, wk, wv, nc, new_wk, s_sc, s_sc0) = _inproj_prompt(
        xp, g_pre, lw["w"], lw["cw"], lw["ap"], cols(qa_s), cols(ka_s), to_minor(win_k), stab, self_tab)
    o_dn, s_fin = _delta_prompt(dn, ba, lw["gn"], tb=DELTA_TB)
    o_at = _attn_prompt(qa, ka, va, tables)
    yp, new_wv, o_at_s = _post(xp, o_dn, o_at, pp, *post_w, window=(s_sc, s_sc0, cols(va_s), to_minor(win_v)))

    o_at_s = (o_at_s.transpose(0, 2, 1).reshape(Bs // 16, 16, AT_PAIRS, LANES).transpose(2, 0, 1, 3)
              .reshape(1, AT_PAIRS, Bs // 16, 16 * LANES).astype(BF16))
    ys = _post(xs2[None], o_dn_s.reshape(1, Bs, DN_W).astype(BF16), o_at_s, ps.reshape(1, Bs, -1), *post_w)

    n_keep = wk.shape[2]
    return (yp, ys.reshape(Bs, 1, D),
            nc[:, 8 - (CONV_W - 1):], s_fin,
            from_minor(wk.reshape(B, AT_HEADS, AT_HD, n_keep)), from_minor(wv.reshape(B, AT_HEADS, AT_HD, n_keep)),
            nc_s.transpose(1, 0, 2), s_new, from_minor(new_wk), from_minor(new_wv))


def kernel(x_prompt, x_sample, cache_conv, state_delta, cache_win_k, cache_win_v, p_prompt, p_sample, rel_bias, g_pre_mix, w_in, conv_w, a_log, dt_bias, g_dn_out, w_out, g_post_mix, g_pre_ffn, w_gate_up, w_down, g_post_ffn, w_ple_proj, w_ple_gate, g_ple):
    depth = w_in.shape[0]
    tables = _band_tables(rel_bias)
    stab, self_tab = _sample_tables(rel_bias)
    yp, ys = x_prompt, x_sample
    outs = [[] for _ in range(8)]
    for i in range(depth):
        lw = _layer_weights(w_in[i], conv_w[i], a_log[i], dt_bias[i], g_dn_out[i], w_out[i], g_post_mix[i],
                            g_pre_ffn[i], w_gate_up[i], w_down[i], g_post_ffn[i], w_ple_proj[i], w_ple_gate[i],
                            g_ple[i])
        g_pre = g_pre_mix[i].astype(F32)[None]
        yp, ys, *rest = _layer(yp, ys, p_prompt[i], p_sample[i], cache_conv[i], state_delta[i], cache_win_k[i],
                               cache_win_v[i], g_pre, lw, tables, stab, self_tab)
        for lst, val in zip(outs, rest):
            lst.append(val)
    return (yp, ys) + tuple(jnp.stack(o) for o in outs)
```

```python
import functools
import math

import numpy as np
import jax
import jax.numpy as jnp
from jax import lax
from jax.experimental import pallas as pl
from jax.experimental.pallas import tpu as pltpu

DN_HEADS = 4
DN_D = 128
CONV_W = 4
AT_HEADS = 8
AT_HD = 64
AT_PAIRS = AT_HEADS // 2
DIL_PAIRS = ((128, 1), (512, 4), (2048, 16))
WIN_MAX = 2048
BAND = 128
NUM_BUCKETS = 32
MAX_DIST = 2048
RMS_EPS = 1e-6
NEG = -1e30
LOG2E = math.log2(math.e)

DN_W = DN_HEADS * DN_D
CONV_CH = 3 * DN_W
AT_W = AT_HEADS * AT_HD
LANES = 128
SUPER = 16 * BAND

V7X_VMEM_BYTES = 64 * 1024 * 1024
VMEM_LIMIT = V7X_VMEM_BYTES - 8 * 1024 * 1024

INPROJ_SUB = 256
DELTA_TB = 512
POST_TM = 512
POST_SUB = 128
FFN_CHUNK = 256

F32 = jnp.float32
BF16 = jnp.bfloat16


def _t5_bucket(n):
    max_exact = NUM_BUCKETS // 2
    n = np.asarray(n)
    large = max_exact + (np.log(np.maximum(n, 1) / max_exact) / math.log(MAX_DIST / max_exact)
                         * (NUM_BUCKETS - max_exact)).astype(np.int32)
    large = np.minimum(large, NUM_BUCKETS - 1)
    return np.where(n < max_exact, n, large).astype(np.int32)


def _sigmoid(x):
    return 1.0 / (1.0 + jnp.exp(-x))


def _silu(x):
    return x * _sigmoid(x)


def _softplus(x):
    return jnp.maximum(x, 0.0) + jnp.log1p(jnp.exp(-jnp.abs(x)))


def _rms(x, g):
    return x * lax.rsqrt(jnp.mean(x * x, axis=-1, keepdims=True) + RMS_EPS) * g


def _dot(a, b):
    return jnp.dot(a, b, preferred_element_type=F32)


def _dot_nt(a, b):
    return lax.dot_general(a, b, (((1,), (1,)), ((), ())), preferred_element_type=F32)


def _const_spec(shape):
    nd = len(shape)
    return pl.BlockSpec(shape, lambda *_: (0,) * nd, pipeline_mode=pl.Buffered(1))


C_CONV = 0
C_Z = CONV_CH
C_Q = C_Z + DN_W
C_K = C_Q + AT_W
C_V = C_K + AT_W
C_BA = C_V + AT_W
IN_COLS_PAD = C_BA + LANES


def _dn_activations(c, ba_raw, ap_ref):
    c = _silu(c)
    qs, ks = [], []
    for h in range(DN_HEADS):
        qh = c[:, h * DN_D:(h + 1) * DN_D]
        kh = c[:, DN_W + h * DN_D:DN_W + (h + 1) * DN_D]
        qs.append(qh * lax.rsqrt(jnp.sum(qh * qh, axis=-1, keepdims=True) + 1e-6) * (DN_D ** -0.5))
        ks.append(kh * lax.rsqrt(jnp.sum(kh * kh, axis=-1, keepdims=True) + 1e-6))
    q = jnp.concatenate(qs, axis=1)
    k = jnp.concatenate(ks, axis=1)
    v = c[:, 2 * DN_W:]
    lane = lax.broadcasted_iota(jnp.int32, ba_raw.shape, 1)
    beta = _sigmoid(ba_raw)
    gdec = -jnp.exp(ap_ref[0:1, :]) * _softplus(ba_raw + ap_ref[1:2, :])
    ba = jnp.where(lane < DN_HEADS, beta, gdec)
    return q, k, v, ba


def _inproj_prompt_kernel(x_ref, g_ref, w_ref, cw_ref, ap_ref, sq_ref, skn_ref, swin_ref, stab_ref, stab0_ref,
                          dn_ref, ba_ref, qa_ref, ka_ref, va_ref,
                          wk_ref, wv_ref, nc_ref, swin_out_ref, ss_ref, ss0_ref, buf_ref, rl_ref, *, tm):
    i = pl.program_id(1)

    @pl.when(i == 0)
    def _():
        buf_ref[0:8, :] = jnp.zeros((8, CONV_CH), F32)

    sub = min(tm, INPROJ_SUB)
    subs = [slice(lo, lo + sub) for lo in range(0, tm, sub)]
    hs = [_rms(x_ref[r, :], g_ref[...]).astype(BF16) for r in subs]
    us = [_dot(h, w_ref[:, C_CONV:C_Z]) for h in hs]
    for r, u in zip(subs, us):
        buf_ref[8 + r.start:8 + r.stop, :] = u
    bas = [_dot(h, w_ref[:, C_BA:C_BA + LANES]) for h in hs]
    ctx = CONV_W - 1
    for r, h, u, ba_raw in zip(subs, hs, us, bas):
        c = u * cw_ref[ctx:CONV_W, :]
        for t in range(ctx):
            c = c + buf_ref[8 - ctx + t + r.start:8 - ctx + t + r.stop, :] * cw_ref[t:t + 1, :]
        q, k, v, ba = _dn_activations(c, ba_raw, ap_ref)
        dn_ref[r, 0:DN_W] = q.astype(BF16)
        dn_ref[r, DN_W:2 * DN_W] = k.astype(BF16)
        dn_ref[r, 2 * DN_W:3 * DN_W] = v.astype(BF16)
        ba_ref[r, :] = ba
        dn_ref[r, 3 * DN_W:4 * DN_W] = _dot(h, w_ref[:, C_Z:C_Q]).astype(BF16)

        qa = _dot(h, w_ref[:, C_Q:C_K]) * (AT_HD ** -0.5 * LOG2E)
        ka = _dot(h, w_ref[:, C_K:C_V])
        va = _dot(h, w_ref[:, C_V:C_BA])

        wk_ref[:, r] = ka.T
        wv_ref[:, r] = va.T

        for t_idx, (val, dst_ref) in enumerate(((qa, qa_ref), (ka, ka_ref), (va, va_ref))):
            for p in range(AT_PAIRS):
                slab = t_idx * AT_PAIRS + p
                for g in range(sub // 16):
                    m = (r.start + 16 * g) // 16
                    for half in range(2):
                        rows = slice(16 * g + 8 * half, 16 * g + 8 * half + 8)
                        rl_ref[slab, pl.ds(33 * m + 16 * half, 8, stride=2), :] = val[rows, p * LANES:(p + 1) * LANES]
    n_m = tm // 16
    for t_idx, dst_ref in enumerate((qa_ref, ka_ref, va_ref)):
        for p in range(AT_PAIRS):
            slab = t_idx * AT_PAIRS + p
            for c in range(16):
                piece = jnp.concatenate(
                    [rl_ref[slab, pl.ds(33 * 8 * blk + 2 * c, 8, stride=33), :] for blk in range(n_m // 8)], axis=0)
                dst_ref[p, :, c * LANES:(c + 1) * LANES] = piece.astype(BF16)
    last = us[-1][sub - 8:sub, :]
    buf_ref[0:8, :] = last
    nc_ref[...] = last

    _window_keys(sq_ref, skn_ref, swin_ref, stab_ref, stab0_ref, swin_out_ref, ss_ref, ss0_ref)


def _inproj_prompt(x, g, w, cw, ap, s_q, s_kn, s_win_k, s_tab, s_tab0):
    B, S, D = x.shape
    Bs, H, E, W = s_win_k.shape
    n_keep = min(WIN_MAX, S)
    tm = B * S // Bs
    assert B * S == Bs * tm and S % tm == 0 and n_keep % tm == 0 and tm % 128 == 0
    nt = S // tm
    first_keep = (S - n_keep) // tm
    col, slab, score, score0 = _window_specs(nt)
    tok = lambda width: pl.BlockSpec((None, tm, width), lambda b, i: (b, i, 0))
    pair = pl.BlockSpec((None, AT_PAIRS, tm // 16, 16 * LANES), lambda b, i: (b, 0, i, 0))
    keep = pl.BlockSpec((None, AT_W, tm), lambda b, i: (b, 0, jnp.maximum(i - first_keep, 0)))
    bf = lambda width: jax.ShapeDtypeStruct((B, S, width), BF16)
    pairs = jax.ShapeDtypeStruct((B, AT_PAIRS, S // 16, 16 * LANES), BF16)
    relayout_rows = -(-33 * (tm // 16) // 8) * 8
    return pl.pallas_call(
        functools.partial(_inproj_prompt_kernel, tm=tm),
        grid=(B, nt),
        in_specs=[tok(D), _const_spec((1, D)), _const_spec(w.shape), _const_spec(cw.shape),
                  _const_spec(ap.shape), col, col, slab, _const_spec(s_tab.shape), _const_spec(s_tab0.shape)],
        out_specs=[tok(4 * DN_W), tok(LANES), pair, pair, pair,
                   keep, keep, pl.BlockSpec((None, 8, CONV_CH), lambda b, i: (b, 0, 0)),
                   slab, score, score0],
        out_shape=[bf(4 * DN_W), jax.ShapeDtypeStruct((B, S, LANES), F32),
                   pairs, pairs, pairs,
                   jax.ShapeDtypeStruct((B, AT_W, n_keep), F32), jax.ShapeDtypeStruct((B, AT_W, n_keep), F32),
                   jax.ShapeDtypeStruct((B, 8, CONV_CH), F32),
                   jax.ShapeDtypeStruct(s_win_k.shape, F32), jax.ShapeDtypeStruct((Bs, H, W), F32),
                   jax.ShapeDtypeStruct((Bs, H, LANES), F32)],
        scratch_shapes=[pltpu.VMEM((tm + 8, CONV_CH), F32),
                        pltpu.VMEM((3 * AT_PAIRS, relayout_rows, LANES), F32)],
        compiler_params=pltpu.CompilerParams(dimension_semantics=("arbitrary", "arbitrary"),
                                             vmem_limit_bytes=VMEM_LIMIT),
        name="inproj_prompt",
    )(x, g, w, cw, ap, s_q, s_kn, s_win_k, s_tab, s_tab0)


def _inproj_sample_kernel(x_ref, g_ref, w_ref, cw_ref, ap_ref, cc_ref,
                          qt_ref, kt_ref, v_ref, z_ref, ba_ref, qa_ref, ka_ref, va_ref, nc_ref):
    x = x_ref[...]
    h = _rms(x, g_ref[...]).astype(BF16)
    u = _dot(h, w_ref[:, C_CONV:C_Z])
    c = u * cw_ref[CONV_W - 1:CONV_W, :]
    for t in range(CONV_W - 1):
        c = c + cc_ref[t] * cw_ref[t:t + 1, :]
    for t in range(1, CONV_W - 1):
        nc_ref[t - 1] = cc_ref[t]
    nc_ref[CONV_W - 2] = u

    ba_raw = _dot(h, w_ref[:, C_BA:C_BA + LANES])
    q, k, v, ba = _dn_activations(c, ba_raw, ap_ref)
    qt_ref[...] = q.T
    kt_ref[...] = k.T
    v_ref[...] = v
    ba_ref[...] = ba
    z_ref[...] = _dot(h, w_ref[:, C_Z:C_Q])
    qa_ref[...] = _dot(h, w_ref[:, C_Q:C_K]) * (AT_HD ** -0.5)
    ka_ref[...] = _dot(h, w_ref[:, C_K:C_V])
    va_ref[...] = _dot(h, w_ref[:, C_V:C_BA])


def _inproj_sample(x, g, w, cw, ap, conv_ctx):
    B, D = x.shape
    f = lambda *s: jax.ShapeDtypeStruct(s, F32)
    return pl.pallas_call(
        _inproj_sample_kernel,
        out_shape=[f(DN_W, B), f(DN_W, B), f(B, DN_W), f(B, DN_W), f(B, LANES),
                   f(B, AT_W), f(B, AT_W), f(B, AT_W), f(CONV_W - 1, B, CONV_CH)],
        compiler_params=pltpu.CompilerParams(vmem_limit_bytes=VMEM_LIMIT),
        name="inproj_sample",
    )(x, g, w, cw, ap, conv_ctx)


DN_CHUNK = 64


def _delta_prompt_kernel(dn_ref, ba_ref, gn_ref, o_ref, s_out_ref, s_ref, *, nb, tb):
    i = pl.program_id(0)
    K0, V0, Z0 = DN_W, 2 * DN_W, 3 * DN_W
    C = DN_CHUNK
    nc = tb // C

    @pl.when(i == 0)
    def _():
        s_ref[...] = jnp.zeros(s_ref.shape, F32)

    rt = lax.broadcasted_iota(jnp.int32, (tb, tb), 0)
    ct = lax.broadcasted_iota(jnp.int32, (tb, tb), 1)
    tri = jnp.where((rt >= ct) & ((rt // C) == (ct // C)), 1.0, 0.0).astype(F32)
    ri = lax.broadcasted_iota(jnp.int32, (C, C), 0)
    ci = lax.broadcasted_iota(jnp.int32, (C, C), 1)
    eye = jnp.where(ri == ci, 1.0, 0.0).astype(F32)
    gn = gn_ref[...]

    units = [(b, h, c) for b in range(nb) for h in range(DN_HEADS) for c in range(nc)]
    pre = {}
    for b in range(nb):
        ba = ba_ref[b]
        gc_all = jnp.dot(tri, ba, preferred_element_type=F32, precision=lax.Precision.HIGHEST)
        pre[b] = (ba, gc_all, gc_all.T, jnp.exp(gc_all))
    kts = {(b, h): dn_ref[b, :, K0 + h * DN_D:K0 + (h + 1) * DN_D].astype(F32).T
           for b in range(nb) for h in range(DN_HEADS)}

    st = {}
    for (b, h, c) in units:
        ba, gc_all, gc_t, eg_all = pre[b]
        rows = slice(c * C, (c + 1) * C)
        lanes = slice(h * DN_D, (h + 1) * DN_D)
        gl = DN_HEADS + h
        kf = dn_ref[b, rows, K0 + h * DN_D:K0 + (h + 1) * DN_D].astype(F32)
        qf = dn_ref[b, rows, lanes].astype(F32)
        vf = dn_ref[b, rows, V0 + h * DN_D:V0 + (h + 1) * DN_D].astype(F32)
        beta = ba[rows, h:h + 1]
        gcol = gc_all[rows, gl:gl + 1]
        grow = gc_t[gl:gl + 1, rows]
        egc = eg_all[rows, gl:gl + 1]
        glast = gc_all[c * C + C - 1:c * C + C, gl:gl + 1]
        kb = kf * beta
        st[b, h, c] = dict(
            dec=jnp.exp(jnp.where(ri >= ci, gcol - grow, NEG)),
            lhs=jnp.concatenate([kb, qf], axis=0).astype(BF16),
            kbf=kf.astype(BF16),
            rhs=jnp.concatenate([vf * beta, kb * egc], axis=1).astype(BF16),
            qdec=(qf * egc).astype(BF16),
            kdt=(kts[b, h][:, rows] * jnp.exp(glast - grow)).astype(BF16),
            glast=jnp.exp(glast))
    for u in units:
        st[u]["both"] = _dot_nt(st[u]["lhs"], st[u]["kbf"])
    for u in units:
        d = st[u]
        d["a"] = jnp.where(ri > ci, d["both"][:C] * d["dec"], 0.0)
        d["intra"] = (d["both"][C:] * d["dec"]).astype(BF16)

    pairs = [(b, h, c) for b in range(nb) for h in range(0, DN_HEADS, 2) for c in range(nc)]
    left = lax.broadcasted_iota(jnp.int32, (1, 2 * C), 1) < C
    eye2 = jnp.concatenate([eye, eye], axis=1)

    def block_diag(m):
        mb = m.astype(BF16)
        zero = jnp.zeros_like(mb)
        return jnp.concatenate([jnp.where(left, mb, zero), jnp.where(left, zero, mb)], axis=0)

    pst = {}
    for (b, h, c) in pairs:
        a2 = jnp.concatenate([st[b, h, c]["a"], st[b, h + 1, c]["a"]], axis=1)
        pst[b, h, c] = dict(xm=eye2 - a2, pm=-a2)
    for pr in pairs:
        d = pst[pr]
        d["pm"] = _dot(d["pm"].astype(BF16), block_diag(d["pm"]))
    for _ in range(4):
        for pr in pairs:
            d = pst[pr]
            d["r"] = _dot(jnp.concatenate([d["xm"], d["pm"]], axis=0).astype(BF16), block_diag(d["pm"]))
        for pr in pairs:
            d = pst[pr]
            d["xm"] = d["xm"] + d["r"][:C]
            d["pm"] = d["r"][C:]
    for pr in pairs:
        d = pst[pr]
        d["r"] = _dot(d["xm"].astype(BF16), block_diag(d["pm"]))
    for (b, h, c) in pairs:
        d = pst[b, h, c]
        inv = (d["xm"] + d["r"]).astype(BF16)
        st[b, h, c]["inv"] = inv[:, :C]
        st[b, h + 1, c]["inv"] = inv[:, C:]
    for u in units:
        d = st[u]
        d["uw"] = _dot(d["inv"], d["rhs"])

    chains = [(b, h) for b in range(nb) for h in range(DN_HEADS)]
    state = {bh: s_ref[bh[0], bh[1]] for bh in chains}
    for c in range(nc):
        rows = slice(c * C, (c + 1) * C)
        r2 = {}
        for (b, h) in chains:
            d = st[b, h, c]
            lhs = jnp.concatenate([d["uw"][:, DN_D:].astype(BF16), d["qdec"]], axis=0)
            r2[b, h] = _dot(lhs, state[b, h].astype(BF16))
        vb = {bh: (st[bh[0], bh[1], c]["uw"][:, :DN_D] - r2[bh][:C]).astype(BF16) for bh in chains}
        for (b, h) in chains:
            d = st[b, h, c]
            o = r2[b, h][C:] + _dot(d["intra"], vb[b, h])
            state[b, h] = state[b, h] * d["glast"] + _dot(d["kdt"], vb[b, h])
            lanes = slice(h * DN_D, (h + 1) * DN_D)
            zf = dn_ref[b, rows, Z0 + h * DN_D:Z0 + (h + 1) * DN_D].astype(F32)
            o_ref[b, rows, lanes] = (_rms(o, gn) * _silu(zf)).astype(BF16)
    for (b, h) in chains:
        s_ref[b, h] = state[b, h]

    @pl.when(i == pl.num_programs(0) - 1)
    def _():
        s_out_ref[...] = s_ref[...]


def _delta_prompt(dn, ba, gn, *, tb):
    B, S, _ = dn.shape
    assert S % tb == 0 and tb % DN_CHUNK == 0
    tok = lambda width: pl.BlockSpec((B, tb, width), lambda i: (0, i, 0))
    st = pl.BlockSpec((B, DN_HEADS, DN_D, DN_D), lambda i: (0, 0, 0, 0))
    return pl.pallas_call(
        functools.partial(_delta_prompt_kernel, nb=B, tb=tb),
        grid=(S // tb,),
        in_specs=[tok(4 * DN_W), tok(LANES), _const_spec((1, DN_D))],
        out_specs=[tok(DN_W), st],
        out_shape=[jax.ShapeDtypeStruct((B, S, DN_W), BF16),
                   jax.ShapeDtypeStruct((B, DN_HEADS, DN_D, DN_D), F32)],
        scratch_shapes=[pltpu.VMEM((B, DN_HEADS, DN_D, DN_D), F32)],
        compiler_params=pltpu.CompilerParams(dimension_semantics=("arbitrary",),
                                             vmem_limit_bytes=VMEM_LIMIT),
        name="delta_prompt",
    )(dn, ba, gn)


SAMPLE_SEQS_PER_STEP = 8


def _delta_sample_kernel(qt_ref, kt_ref, v_ref, z_ref, ba_ref, gn_ref, s_ref, o_ref, s_out_ref):
    step = pl.program_id(0)
    nb = qt_ref.shape[1]
    per = v_ref.shape[0]
    lane = lax.broadcasted_iota(jnp.int32, (DN_D, nb), 1)
    gn = gn_ref[...]
    units = [(j, h) for j in range(per) for h in range(DN_HEADS)]
    rows = {h: slice(h * DN_D, (h + 1) * DN_D) for h in range(DN_HEADS)}
    sel = {j: lane == step * per + j for j in range(per)}
    qcol = {(j, h): jnp.sum(jnp.where(sel[j], qt_ref[rows[h], :], 0.0), axis=1, keepdims=True)
            for (j, h) in units}
    kcol = {(j, h): jnp.sum(jnp.where(sel[j], kt_ref[rows[h], :], 0.0), axis=1, keepdims=True)
            for (j, h) in units}
    egd = jnp.exp(ba_ref[...])
    s = {(j, h): s_ref[j, h] * egd[j:j + 1, DN_HEADS + h:DN_HEADS + h + 1] for (j, h) in units}
    kv = {u: jnp.sum(kcol[u] * s[u], axis=0, keepdims=True) for u in units}
    for (j, h) in units:
        delta = (v_ref[j:j + 1, rows[h]] - kv[j, h]) * ba_ref[j:j + 1, h:h + 1]
        s[j, h] = s[j, h] + kcol[j, h] * delta
        s_out_ref[j, h] = s[j, h]
    o = {u: jnp.sum(qcol[u] * s[u], axis=0, keepdims=True) for u in units}
    for (j, h) in units:
        o_ref[j:j + 1, rows[h]] = _rms(o[j, h], gn) * _silu(z_ref[j:j + 1, rows[h]])


def _delta_sample(qt, kt, v, z, ba, gn, state):
    B = v.shape[0]
    per = SAMPLE_SEQS_PER_STEP
    assert B % per == 0
    row = lambda width: pl.BlockSpec((per, width), lambda b: (b, 0))
    st = pl.BlockSpec((per, DN_HEADS, DN_D, DN_D), lambda b: (b, 0, 0, 0))
    return pl.pallas_call(
        _delta_sample_kernel,
        grid=(B // per,),
        in_specs=[_const_spec(qt.shape), _const_spec(kt.shape), row(DN_W), row(DN_W), row(LANES),
                  _const_spec((1, DN_D)), st],
        out_specs=[row(DN_W), st],
        out_shape=[jax.ShapeDtypeStruct((B, DN_W), F32), jax.ShapeDtypeStruct(state.shape, F32)],
        compiler_params=pltpu.CompilerParams(dimension_semantics=("arbitrary",)),
        name="delta_sample",
    )(qt, kt, v, z, ba, gn, state)


def _stack_position(d):
    a = np.arange(BAND)
    rows = 8 * d
    return (16 // d) * (a % rows) + a // rows


def _bias_lookup(rel_bias, bucket):
    onehot = jnp.asarray(bucket[..., None] == np.arange(NUM_BUCKETS), dtype=BF16).astype(F32)
    return jnp.einsum("...n,nh->h...", onehot, rel_bias.astype(F32), precision=lax.Precision.HIGHEST)


def _band_tables(rel_bias):
    tables = []
    for w, d in DIL_PAIRS:
        assert w // d == BAND
        pos = _stack_position(d)
        dist = pos[:, None] - np.concatenate([pos - BAND, pos])[None, :]
        valid = (dist >= 0) & (dist <= BAND)
        bucket = _t5_bucket(np.clip(dist, 0, BAND) * d)
        bias = jnp.where(valid[None], _bias_lookup(rel_bias, bucket) * LOG2E, NEG)
        tables.append(bias.reshape(AT_PAIRS, 2 * BAND, 2 * BAND))
    return jnp.stack(tables)


def _attn_unit(q, kk, vv, table, pen):
    lane = lax.broadcasted_iota(jnp.int32, (1, LANES), 1)
    first = lane < AT_HD
    zero = jnp.zeros_like(q)
    q2 = jnp.concatenate([jnp.where(first, q, zero), jnp.where(first, zero, q)], axis=0)
    s = _dot_nt(q2, kk) + table
    if pen is not None:
        s = s + pen
    m = jnp.max(s, axis=1, keepdims=True)
    p = jnp.exp2(s - m)
    l = jnp.sum(p, axis=1, keepdims=True)
    pv = _dot(p.astype(BF16), vv)
    acc = jnp.where(first, pv[:BAND], pv[BAND:])
    mt = jnp.where(first, m[:BAND], m[BAND:])
    lt = jnp.where(first, l[:BAND], l[BAND:])
    return acc, mt, lt


def _attn_prompt_kernel(q_ref, k_ref, v_ref, t_ref, o_ref,
                        k16, v16, q4, k4, v4, q1, k1, v1, acc_ref, m_ref, l_ref):
    n = pl.program_id(2)

    @pl.when(n == 0)
    def _():
        zb = jnp.zeros((BAND, LANES), BF16)
        for rho in range(16):
            k16[rho, 0:BAND, :] = zb
            v16[rho, 0:BAND, :] = zb
        for rho in range(4):
            k4[rho, 0:BAND, :] = zb
            v4[rho, 0:BAND, :] = zb
        k1[0:BAND, :] = zb
        v1[0:BAND, :] = zb

    @pl.when(n > 0)
    def _():
        for rho in range(16):
            k16[rho, 0:BAND, :] = k16[rho, BAND:2 * BAND, :]
            v16[rho, 0:BAND, :] = v16[rho, BAND:2 * BAND, :]
        for rho in range(4):
            k4[rho, 0:BAND, :] = k4[rho, 4 * BAND:5 * BAND, :]
            v4[rho, 0:BAND, :] = v4[rho, 4 * BAND:5 * BAND, :]
        k1[0:BAND, :] = k1[SUPER:SUPER + BAND, :]
        v1[0:BAND, :] = v1[SUPER:SUPER + BAND, :]

    for rho in range(16):
        cols = slice(rho * LANES, (rho + 1) * LANES)
        k16[rho, BAND:2 * BAND, :] = k_ref[:, cols]
        v16[rho, BAND:2 * BAND, :] = v_ref[:, cols]
    for rho in range(4):
        for blk in range(4):
            for j in range(4):
                src = (slice(32 * blk, 32 * blk + 32), slice((4 * j + rho) * LANES, (4 * j + rho + 1) * LANES))
                dst = slice(BAND * blk + 32 * j, BAND * blk + 32 * j + 32)
                q4[rho, dst, :] = q_ref[src]
                k4[rho, BAND + dst.start:BAND + dst.stop, :] = k_ref[src]
                v4[rho, BAND + dst.start:BAND + dst.stop, :] = v_ref[src]
    for src_ref, dst_ref, off in ((q_ref, q1, 0), (k_ref, k1, BAND), (v_ref, v1, BAND)):
        for gam in range(8):
            lo, hi = [], []
            for r in range(16):
                piece = src_ref[16 * gam:16 * gam + 16, r * LANES:(r + 1) * LANES].astype(F32)
                lo.append(piece[0:8])
                hi.append(piece[8:16])
            dst_ref[off + BAND * (2 * gam):off + BAND * (2 * gam + 1), :] = jnp.concatenate(lo, axis=0).astype(BF16)
            dst_ref[off + BAND * (2 * gam + 1):off + BAND * (2 * gam + 2), :] = jnp.concatenate(hi, axis=0).astype(BF16)

    lane2 = lax.broadcasted_iota(jnp.int32, (1, 2 * BAND), 1)
    pen = jnp.where((lane2 < BAND) & (n == 0), NEG, 0.0).astype(F32)

    t16 = t_ref[2]
    for rho in range(16):
        cols = slice(rho * LANES, (rho + 1) * LANES)
        acc, mt, lt = _attn_unit(q_ref[:, cols], k16[rho], v16[rho], t16, pen)
        acc_ref[2, :, cols] = acc
        m_ref[2, :, cols] = mt
        l_ref[2, :, cols] = lt

    def park(g, rows, cols, acc, mt, lt):
        acc_ref[g, rows, cols] = acc
        m_ref[g, rows, cols] = mt
        l_ref[g, rows, cols] = lt

    t4 = t_ref[1]
    for rho in range(4):
        for blk in range(4):
            acc, mt, lt = _attn_unit(q4[rho, BAND * blk:BAND * (blk + 1), :],
                                     k4[rho, BAND * blk:BAND * (blk + 2), :],
                                     v4[rho, BAND * blk:BAND * (blk + 2), :],
                                     t4, pen if blk == 0 else None)
            for j in range(4):
                sub = slice(32 * j, 32 * j + 32)
                park(1, slice(32 * blk, 32 * blk + 32), slice((4 * j + rho) * LANES, (4 * j + rho + 1) * LANES),
                     acc[sub], mt[sub], lt[sub])

    t1 = t_ref[0]
    for blk in range(16):
        acc, mt, lt = _attn_unit(q1[BAND * blk:BAND * (blk + 1), :],
                                 k1[BAND * blk:BAND * (blk + 2), :],
                                 v1[BAND * blk:BAND * (blk + 2), :],
                                 t1, pen if blk == 0 else None)
        for r in range(16):
            sub = slice(8 * r, 8 * r + 8)
            park(0, slice(8 * blk, 8 * blk + 8), slice(r * LANES, (r + 1) * LANES),
                 acc[sub], mt[sub], lt[sub])

    for c in range(16):
        cols = slice(c * LANES, (c + 1) * LANES)
        ms = [m_ref[g, :, cols] for g in range(3)]
        m_all = jnp.maximum(jnp.maximum(ms[0], ms[1]), ms[2])
        w = [jnp.exp2(m - m_all) for m in ms]
        num = sum(acc_ref[g, :, cols] * w[g] for g in range(3))
        den = sum(l_ref[g, :, cols] * w[g] for g in range(3))
        o_ref[:, cols] = (num / den).astype(BF16)


def _attn_prompt(q, k, v, tables):
    B, P, S16, _ = q.shape
    S = 16 * S16
    assert S % SUPER == 0
    nt = S // SUPER
    tile = pl.BlockSpec((None, None, BAND, 16 * LANES), lambda b, p, n: (b, p, n, 0))
    tab = pl.BlockSpec((3, None, 2 * BAND, 2 * BAND), lambda b, p, n: (0, p, 0, 0))
    vm = lambda *s: pltpu.VMEM(s, BF16)
    out = pl.pallas_call(
        _attn_prompt_kernel,
        grid=(B, P, nt),
        in_specs=[tile, tile, tile, tab],
        out_specs=tile,
        out_shape=jax.ShapeDtypeStruct((B, P, S // 16, 16 * LANES), BF16),
        scratch_shapes=[vm(16, 2 * BAND, LANES), vm(16, 2 * BAND, LANES),
                        vm(4, 4 * BAND, LANES), vm(4, 5 * BAND, LANES), vm(4, 5 * BAND, LANES),
                        vm(SUPER, LANES), vm(SUPER + BAND, LANES), vm(SUPER + BAND, LANES),
                        pltpu.VMEM((3, BAND, 16 * LANES), F32), pltpu.VMEM((3, BAND, 16 * LANES), F32),
                        pltpu.VMEM((3, BAND, 16 * LANES), F32)],
        compiler_params=pltpu.CompilerParams(dimension_semantics=("arbitrary", "arbitrary", "arbitrary"),
                                             vmem_limit_bytes=VMEM_LIMIT),
        name="attn_prompt",
    )(q, k, v, tables)
    return out


def _sample_tables(rel_bias):
    W = WIN_MAX
    j = W - np.arange(W)
    count = sum(((j % d == 0) & (j <= w)).astype(np.int64) for w, d in DIL_PAIRS)
    bias = _bias_lookup(rel_bias, _t5_bucket(j))
    tab = jnp.where(count[None] > 0, bias + np.log(np.maximum(count, 1))[None].astype(np.float32), NEG)
    self_tab = rel_bias[0].astype(F32) + math.log(len(DIL_PAIRS))
    return tab, jnp.broadcast_to(self_tab[:, None], (AT_HEADS, LANES))


def _window_keys(q_ref, kn_ref, k_ref, t_ref, t0_ref, ok_ref, s_ref, s0_ref):
    H, E, W = k_ref.shape
    last = lax.broadcasted_iota(jnp.int32, (1, W), 1) == W - 1
    for h in range(H):
        qc = q_ref[:, h:h + 1]
        knc = kn_ref[:, h:h + 1]
        kh = k_ref[h]
        s_ref[h:h + 1, :] = jnp.sum(kh * qc, axis=0, keepdims=True) + t_ref[h:h + 1, :]
        s0_ref[h:h + 1, :] = jnp.sum(qc * knc, axis=0, keepdims=True) + t0_ref[h:h + 1, :]
        ok_ref[h] = jnp.where(last, knc, pltpu.roll(kh, W - 1, axis=1))


def _window_values(s_ref, s0_ref, vn_ref, v_ref, ov_ref, o_ref):
    H, E, W = v_ref.shape
    last = lax.broadcasted_iota(jnp.int32, (1, W), 1) == W - 1
    s = s_ref[...]
    s0 = s0_ref[:, 0:1]
    m = jnp.maximum(jnp.max(s, axis=1, keepdims=True), s0)
    p = jnp.exp(s - m)
    p0 = jnp.exp(s0 - m)
    l = jnp.sum(p, axis=1, keepdims=True) + p0
    for h in range(H):
        vh = v_ref[h]
        vnc = vn_ref[:, h:h + 1]
        acc = jnp.sum(vh * p[h:h + 1, :], axis=1, keepdims=True) + p0[h:h + 1, :] * vnc
        o_ref[:, h:h + 1] = acc / l[h:h + 1, :]
        ov_ref[h] = jnp.where(last, vnc, pltpu.roll(vh, W - 1, axis=1))


def _window_specs(nt):
    seq = lambda b, i: b * nt + i
    col = pl.BlockSpec((None, AT_HD, AT_HEADS), lambda b, i: (seq(b, i), 0, 0))
    slab = pl.BlockSpec((None, AT_HEADS, AT_HD, WIN_MAX), lambda b, i: (seq(b, i), 0, 0, 0))
    score = pl.BlockSpec((None, AT_HEADS, WIN_MAX), lambda b, i: (seq(b, i), 0, 0))
    score0 = pl.BlockSpec((None, AT_HEADS, LANES), lambda b, i: (seq(b, i), 0, 0))
    return col, slab, score, score0


def _post_kernel(x_ref, od_ref, oa_ref, p_ref, wo_ref, wgu_ref, wd_ref, wpg_ref, wpp_ref, gains_ref, *rest):
    if len(rest) == 2:
        y_ref, rl_ref = rest
    else:
        ss_ref, ss0_ref, svn_ref, swin_ref, y_ref, swin_out_ref, so_ref, rl_ref = rest
        _window_values(ss_ref, ss0_ref, svn_ref, swin_ref, swin_out_ref, so_ref)
    ffn = wd_ref.shape[0]
    tm = x_ref.shape[0]
    sub = min(tm, POST_SUB)
    subs = [slice(lo, lo + sub) for lo in range(0, tm, sub)]
    ns = len(subs)
    n_m = tm // 16
    for p in range(AT_PAIRS):
        for c in range(16):
            piece = oa_ref[p, :, c * LANES:(c + 1) * LANES].astype(F32)
            for blk in range(n_m // 8):
                rl_ref[p, pl.ds(33 * 8 * blk + 2 * c, 8, stride=33), :] = piece[8 * blk:8 * blk + 8]

    def attn_rows(r):
        return [jnp.concatenate([rl_ref[p, pl.ds(33 * m + 16 * half, 8, stride=2), :]
                                 for m in range(r.start // 16, r.stop // 16) for half in range(2)],
                                axis=0).astype(BF16) for p in range(AT_PAIRS)]

    mix = [_dot(jnp.concatenate([od_ref[r, :]] + attn_rows(r), axis=1), wo_ref[...]) for r in subs]
    x1 = [x_ref[r, :] + _rms(m, gains_ref[0:1, :]) for r, m in zip(subs, mix)]
    h2 = [_rms(t, gains_ref[1:2, :]).astype(BF16) for t in x1]

    work = [(lo, s) for s in range(ns) for lo in range(0, ffn, FFN_CHUNK)]
    gate_up = {}

    def issue(lo, s):
        gate_up[lo, s] = (_dot(h2[s], wgu_ref[:, lo:lo + FFN_CHUNK]),
                          _dot(h2[s], wgu_ref[:, ffn + lo:ffn + lo + FFN_CHUNK]))

    issue(*work[0])
    down = [None] * ns
    for idx, (lo, s) in enumerate(work):
        if idx + 1 < len(work):
            issue(*work[idx + 1])
        gt, up = gate_up.pop((lo, s))
        part = _dot((_silu(gt) * up).astype(BF16), wd_ref[lo:lo + FFN_CHUNK, :])
        down[s] = part if down[s] is None else down[s] + part

    x2 = [a + _rms(d, gains_ref[2:3, :]) for a, d in zip(x1, down)]
    gate = [_dot(t.astype(BF16), wpg_ref[...]) for t in x2]
    pe = [_dot(p_ref[r, :].astype(BF16), wpp_ref[...]) for r in subs]
    for r, t, g, e in zip(subs, x2, gate, pe):
        y_ref[r, :] = t + _rms(_sigmoid(g) * e, gains_ref[3:4, :])


def _post(x, o_dn, o_at, p, wo, wgu, wd, wpg, wpp, gains, window=None):
    B, S, D = x.shape
    tm = min(S, POST_TM) if window is None else B * S // window[3].shape[0]
    assert S % tm == 0 and tm % 128 == 0
    nt = S // tm
    tok = lambda width: pl.BlockSpec((None, tm, width), lambda b, i: (b, i, 0))
    in_specs = [tok(D), tok(DN_W),
                pl.BlockSpec((None, AT_PAIRS, tm // 16, 16 * LANES), lambda b, i: (b, 0, i, 0)),
                tok(p.shape[-1]),
                _const_spec(wo.shape), _const_spec(wgu.shape), _const_spec(wd.shape),
                _const_spec(wpg.shape), _const_spec(wpp.shape), _const_spec(gains.shape)]
    out_specs = [tok(D)]
    out_shape = [jax.ShapeDtypeStruct((B, S, D), F32)]
    args = [x, o_dn, o_at, p, wo, wgu, wd, wpg, wpp, gains]
    if window is not None:
        Bs, H, E, W = window[3].shape
        assert B * S == Bs * tm
        col, slab, score, score0 = _window_specs(nt)
        in_specs += [score, score0, col, slab]
        out_specs += [slab, col]
        out_shape += [jax.ShapeDtypeStruct((Bs, H, E, W), F32), jax.ShapeDtypeStruct((Bs, E, H), F32)]
        args += list(window)
    out = pl.pallas_call(
        _post_kernel,
        grid=(B, nt),
        in_specs=in_specs,
        out_specs=out_specs,
        out_shape=out_shape,
        scratch_shapes=[pltpu.VMEM((AT_PAIRS, -(-33 * (tm // 16) // 8) * 8, LANES), F32)],
        compiler_params=pltpu.CompilerParams(dimension_semantics=("arbitrary", "arbitrary"),
                                             vmem_limit_bytes=VMEM_LIMIT),
        name="post",
    )(*args)
    return out[0] if window is None else out


def _layer_weights(w_in, conv_w, a_log, dt_bias, g_dn_out, w_out, g_post_mix, g_pre_ffn, w_gate_up, w_down,
                   g_post_ffn, w_ple_proj, w_ple_gate, g_ple):
    c0 = CONV_CH
    c1 = c0 + DN_W
    c2 = c1 + 2 * DN_HEADS
    w_ba = jnp.pad(w_in[:, c1:c2], ((0, 0), (0, LANES - 2 * DN_HEADS)))
    w = jnp.concatenate([w_in[:, :c1], w_in[:, c2:], w_ba], axis=1).astype(BF16)
    assert w.shape[1] == IN_COLS_PAD
    ap = jnp.zeros((8, LANES), F32)
    ap = ap.at[0, DN_HEADS:2 * DN_HEADS].set(a_log.astype(F32))
    ap = ap.at[1, DN_HEADS:2 * DN_HEADS].set(dt_bias.astype(F32))
    gains = jnp.zeros((8, w_out.shape[1]), F32)
    for r, g in enumerate((g_post_mix, g_pre_ffn, g_post_ffn, g_ple)):
        gains = gains.at[r].set(g.astype(F32))
    return dict(w=w, cw=conv_w.astype(F32), ap=ap, gn=g_dn_out.astype(F32)[None], wo=w_out.astype(BF16),
                wgu=w_gate_up.astype(BF16), wd=w_down.astype(BF16), wpg=w_ple_gate.astype(BF16),
                wpp=w_ple_proj.astype(BF16), gains=gains)


def _layer(xp, xs, pp, ps, cache_conv, state, win_k, win_v, g_pre, lw, tables, stab, self_tab):
    B, S, D = xp.shape
    Bs, T, _ = xs.shape
    assert T == 1
    post_w = (lw["wo"], lw["wgu"], lw["wd"], lw["wpg"], lw["wpp"], lw["gains"])
    to_minor = lambda t: t.transpose(0, 2, 3, 1)
    from_minor = lambda t: t.transpose(0, 3, 1, 2)

    cols = lambda t: t.reshape(Bs, AT_HEADS, AT_HD).transpose(0, 2, 1)

    xs2 = xs.reshape(Bs, D)
    qt, kt, v_s, z_s, ba_s, qa_s, ka_s, va_s, nc_s = _inproj_sample(
        xs2, g_pre, lw["w"], lw["cw"], lw["ap"], cache_conv.transpose(1, 0, 2))
    o_dn_s, s_new = _delta_sample(qt, kt, v_s, z_s, ba_s, lw["gn"], state)

    (dn, ba, qa, ka, va, wk, wv, nc, new_wk, s_sc, s_sc0) = _inproj_prompt(
        xp, g_pre, lw["w"], lw["cw"], lw["ap"], cols(qa_s), cols(ka_s), to_minor(win_k), stab, self_tab)
    o_dn, s_fin = _delta_prompt(dn, ba, lw["gn"], tb=DELTA_TB)
    o_at = _attn_prompt(qa, ka, va, tables)
    yp, new_wv, o_at_s = _post(xp, o_dn, o_at, pp, *post_w, window=(s_sc, s_sc0, cols(va_s), to_minor(win_v)))

    o_at_s = (o_at_s.transpose(0, 2, 1).reshape(Bs // 16, 16, AT_PAIRS, LANES).transpose(2, 0, 1, 3)
              .reshape(1, AT_PAIRS, Bs // 16, 16 * LANES).astype(BF16))
    ys = _post(xs2[None], o_dn_s.reshape(1, Bs, DN_W).astype(BF16), o_at_s, ps.reshape(1, Bs, -1), *post_w)

    n_keep = wk.shape[2]
    return (yp, ys.reshape(Bs, 1, D),
            nc[:, 8 - (CONV_W - 1):], s_fin,
            from_minor(wk.reshape(B, AT_HEADS, AT_HD, n_keep)), from_minor(wv.reshape(B, AT_HEADS, AT_HD, n_keep)),
            nc_s.transpose(1, 0, 2), s_new, from_minor(new_wk), from_minor(new_wv))


def kernel(x_prompt, x_sample, cache_conv, state_delta, cache_win_k, cache_win_v, p_prompt, p_sample, rel_bias, g_pre_mix, w_in, conv_w, a_log, dt_bias, g_dn_out, w_out, g_post_mix, g_pre_ffn, w_gate_up, w_down, g_post_ffn, w_ple_proj, w_ple_gate, g_ple):
    depth = w_in.shape[0]
    tables = _band_tables(rel_bias)
    stab, self_tab = _sample_tables(rel_bias)
    yp, ys = x_prompt, x_sample
    outs = [[] for _ in range(8)]
    for i in range(depth):
        lw = _layer_weights(w_in[i], conv_w[i], a_log[i], dt_bias[i], g_dn_out[i], w_out[i], g_post_mix[i],
                            g_pre_ffn[i], w_gate_up[i], w_down[i], g_post_ffn[i], w_ple_proj[i], w_ple_gate[i],
                            g_ple[i])
        g_pre = g_pre_mix[i].astype(F32)[None]
        yp, ys, *rest = _layer(yp, ys, p_prompt[i], p_sample[i], cache_conv[i], state_delta[i], cache_win_k[i],
                               cache_win_v[i], g_pre, lw, tables, stab, self_tab)
        for lst, val in zip(outs, rest):
            lst.append(val)
    return (yp, ys) + tuple(jnp.stack(o) for o in outs)
```
